```python
import math
import jax, jax.numpy as jnp
from jax import lax
import numpy as np

D_MODEL = 1024
BATCH = 16
SEQ = 4096
DEPTH = 2
DEC_BATCH = 32
DEC_SEQ = 64
PAST_LEN = 2048

CHUNK = 64
N_MEM = 256
EPS = 1e-6
NEG_INF = -1e30

POOL_WIDTH = 768
POOL_WINDOWS = (2, 4, 8, 16)
POOL_GROUPS = 4
POOL_GROUP = POOL_WIDTH // POOL_GROUPS
POOL_HIST = 15

SSM_WIDTH = 768
SSM_GROUP = 16
SSM_GROUPS = SSM_WIDTH // SSM_GROUP
SSM_STATE = 64

ATTN_HEADS = 8
ATTN_HEAD_DIM = 64
ATTN_V_DIM = 2 * ATTN_HEAD_DIM
ATTN_WIDTH = ATTN_HEADS * ATTN_V_DIM
Q_BLOCK = 128

MEM_HEADS = 4
MEM_HEAD_DIM = 64
MEM_WIDTH = MEM_HEADS * MEM_HEAD_DIM

N_BRANCH = 4
IN_WIDTH = 2 * POOL_WIDTH + 2 * SSM_WIDTH + 4 * ATTN_WIDTH + MEM_WIDTH + N_BRANCH * D_MODEL

kernel_name = "gated_hybrid_streaming_encoder_step"


def in_split_idx():
    sizes = (POOL_WIDTH, POOL_WIDTH, SSM_WIDTH, SSM_WIDTH, ATTN_WIDTH, ATTN_WIDTH, ATTN_WIDTH, ATTN_WIDTH,
             MEM_WIDTH, N_BRANCH * D_MODEL)
    idx, acc = [], 0
    for s in sizes[:-1]:
        acc += s
        idx.append(acc)
    return idx


def alibi_slopes():
    return jnp.asarray(2.0 ** (-8.0 * np.arange(1, ATTN_HEADS + 1) / ATTN_HEADS), dtype=jnp.float32)


def rmsnorm(x, g):
    xf = x.astype(jnp.float32)
    y = xf * lax.rsqrt(jnp.mean(xf * xf, axis=-1, keepdims=True) + EPS)
    return (y * g.astype(jnp.float32)).astype(x.dtype)


def pool_mix(u, hist, pos0, w_grp, scale):
    B, T, C = u.shape
    ext = jnp.concatenate([hist, u], axis=1)
    cs = jnp.cumsum(ext.astype(jnp.float32), axis=1)
    cs = jnp.concatenate([jnp.zeros((B, 1, C), jnp.float32), cs], axis=1)
    end = cs[:, POOL_HIST + 1:]
    pos = pos0 + jnp.arange(T, dtype=jnp.int32)
    means = []
    for gi, w in enumerate(POOL_WINDOWS):
        sl = slice(gi * POOL_GROUP, (gi + 1) * POOL_GROUP)
        start = cs[:, POOL_HIST + 1 - w:POOL_HIST + 1 - w + T, sl]
        cnt = jnp.minimum(w, pos + 1).astype(jnp.float32)[None, :, None]
        means.append((end[..., sl] - start) / cnt)
    mean = jnp.stack(means, axis=2)
    diff = (mean - u.astype(jnp.float32).reshape(B, T, POOL_GROUPS, POOL_GROUP)).astype(u.dtype)
    y = jnp.einsum('btgc,gcd->btgd', diff, w_grp).reshape(B, T, C) * scale
    return y, ext[:, -POOL_HIST:]


def ssm_discretise(lam_re, lam_im, log_dt, b_re, b_im):
    dt = jnp.exp(log_dt.astype(jnp.float32))[:, None]
    lr, li = lam_re.astype(jnp.float32), lam_im.astype(jnp.float32)
    mag = jnp.exp(lr * dt)
    ab_re, ab_im = mag * jnp.cos(li * dt), mag * jnp.sin(li * dt)
    nr, ni = ab_re - 1.0, ab_im
    den = lr * lr + li * li
    f_re = (nr * lr + ni * li) / den
    f_im = (ni * lr - nr * li) / den
    br, bi = b_re.astype(jnp.float32), b_im.astype(jnp.float32)
    bb_re = f_re[..., None] * br - f_im[..., None] * bi
    bb_im = f_re[..., None] * bi + f_im[..., None] * br
    return ab_re, ab_im, bb_re, bb_im


def ssm_combine(e1, e2):
    a1r, a1i, b1r, b1i = e1
    a2r, a2i, b2r, b2i = e2
    return (a2r * a1r - a2i * a1i, a2r * a1i + a2i * a1r,
            a2r * b1r - a2i * b1i + b2r, a2r * b1i + a2i * b1r + b2i)


def ssm_block(u, s_re, s_im, ab_re, ab_im, bb_re, bb_im, c_re, c_im):
    L = u.shape[1]
    bu_re = jnp.einsum('gpc,blgc->lbgp', bb_re, u)
    bu_im = jnp.einsum('gpc,blgc->lbgp', bb_im, u)
    bu_re = bu_re.at[0].add(ab_re * s_re - ab_im * s_im)
    bu_im = bu_im.at[0].add(ab_re * s_im + ab_im * s_re)
    a_re = jnp.broadcast_to(ab_re, bu_re.shape)
    a_im = jnp.broadcast_to(ab_im, bu_im.shape)
    _, _, h_re, h_im = lax.associative_scan(ssm_combine, (a_re, a_im, bu_re, bu_im), axis=0)
    y = jnp.einsum('gcp,lbgp->blgc', c_re, h_re) - jnp.einsum('gcp,lbgp->blgc', c_im, h_im)
    return y, h_re[L - 1], h_im[L - 1]


def ssm_mix(u, s_re, s_im, lp):
    B, T, C = u.shape
    uf = u.astype(jnp.float32).reshape(B, T, SSM_GROUPS, SSM_GROUP)
    ab_re, ab_im, bb_re, bb_im = ssm_discretise(lp['lam_re'], lp['lam_im'], lp['log_dt'], lp['b_re'], lp['b_im'])
    c_re, c_im = lp['c_re'].astype(jnp.float32), lp['c_im'].astype(jnp.float32)
    s_re, s_im = s_re.astype(jnp.float32), s_im.astype(jnp.float32)
    if T > CHUNK:
        n = T // CHUNK
        uc = uf.reshape(B, n, CHUNK, SSM_GROUPS, SSM_GROUP).swapaxes(0, 1)

        def step(carry, ub):
            yb, r, i = ssm_block(ub, carry[0], carry[1], ab_re, ab_im, bb_re, bb_im, c_re, c_im)
            return (r, i), yb

        (s_re, s_im), ys = lax.scan(step, (s_re, s_im), uc)
        y = ys.swapaxes(0, 1).reshape(B, T, SSM_GROUPS, SSM_GROUP)
    else:
        y, s_re, s_im = ssm_block(uf, s_re, s_im, ab_re, ab_im, bb_re, bb_im, c_re, c_im)
    y = y.reshape(B, T, C) + lp['ssm_d'].astype(jnp.float32) * uf.reshape(B, T, C)
    y = jax.nn.gelu(y).astype(u.dtype)
    z = jnp.einsum('btc,ce->bte', y, lp['w_glu'])
    out = z[..., :C] * jax.nn.sigmoid(z[..., C:].astype(jnp.float32)).astype(u.dtype)
    return out, s_re.astype(u.dtype), s_im.astype(u.dtype)


def diff_attention(q, q_pos, k, v, k_pos, lam, slopes):
    d = ATTN_HEAD_DIM
    scale = d ** -0.5
    s1 = jnp.einsum('bqhd,bkhd->bhqk', q[..., :d], k[..., :d]).astype(jnp.float32) * scale
    s2 = jnp.einsum('bqhd,bkhd->bhqk', q[..., d:], k[..., d:]).astype(jnp.float32) * scale
    dist = jnp.abs(q_pos[:, None] - k_pos[None, :]).astype(jnp.float32)
    visible = (k_pos[None, :] // CHUNK) <= (q_pos[:, None] // CHUNK)
    bias = jnp.where(visible[None], -slopes[:, None, None] * dist[None], NEG_INF)
    a = jax.nn.softmax(s1 + bias, axis=-1) - lam * jax.nn.softmax(s2 + bias, axis=-1)
    return jnp.einsum('bhqk,bkhd->bqhd', a.astype(v.dtype), v)


def memory_attention(q, mk, mv):
    s = jnp.einsum('bqhd,bmhd->bhqm', q, mk).astype(jnp.float32) * (MEM_HEAD_DIM ** -0.5)
    a = jax.nn.softmax(s, axis=-1)
    return jnp.einsum('bhqm,bmhd->bqhd', a.astype(mv.dtype), mv)


def memory_kv(mem, g, w):
    B, M, _ = mem.shape
    kv = jnp.einsum('bmd,de->bme', rmsnorm(mem, g), w)
    k, v = jnp.split(kv, 2, axis=-1)
    return k.reshape(B, M, MEM_HEADS, MEM_HEAD_DIM), v.reshape(B, M, MEM_HEADS, MEM_HEAD_DIM)


def trunk_layer(x, pool_hist, ssm_re, ssm_im, k_past, v_past, mem_k, mem_v, lp, lambda_init):
    B, T, _ = x.shape
    P = k_past.shape[1]
    h = rmsnorm(x, lp['norm_g'])
    proj = jnp.einsum('btd,de->bte', h, lp['w_in'])
    pu, pz, su, sz, q, k, v, az, mq, gates = jnp.split(proj, in_split_idx(), axis=-1)
    pool_y, pool_hist_new = pool_mix(pu, pool_hist, P, lp['pool_w'], lp['pool_scale'])
    pool_out = jnp.einsum('btc,cd->btd', pool_y * jax.nn.silu(pz), lp['w_br_pool'])
    ssm_y, ssm_re_new, ssm_im_new = ssm_mix(su, ssm_re, ssm_im, lp)
    ssm_out = jnp.einsum('btc,cd->btd', ssm_y * jax.nn.silu(sz), lp['w_br_ssm'])
    qh = q.reshape(B, T, ATTN_HEADS, ATTN_V_DIM)
    kh = k.reshape(B, T, ATTN_HEADS, ATTN_V_DIM)
    vh = v.reshape(B, T, ATTN_HEADS, ATTN_V_DIM)
    k_all = jnp.concatenate([k_past, kh], axis=1)
    v_all = jnp.concatenate([v_past, vh], axis=1)
    k_pos = jnp.arange(P + T, dtype=jnp.int32)
    q_pos = P + jnp.arange(T, dtype=jnp.int32)
    lf = lambda a: a.astype(jnp.float32)
    lam = jnp.exp(jnp.sum(lf(lp['lq1']) * lf(lp['lk1']))) - jnp.exp(jnp.sum(lf(lp['lq2']) * lf(lp['lk2']))) + lambda_init
    slopes = alibi_slopes()
    if T > Q_BLOCK:
        nb = T // Q_BLOCK
        qb = qh.reshape(B, nb, Q_BLOCK, ATTN_HEADS, ATTN_V_DIM).swapaxes(0, 1)
        pb = q_pos.reshape(nb, Q_BLOCK)
        o = lax.map(lambda qp: diff_attention(qp[0], qp[1], k_all, v_all, k_pos, lam, slopes), (qb, pb))
        o = o.swapaxes(0, 1).reshape(B, T, ATTN_HEADS, ATTN_V_DIM)
    else:
        o = diff_attention(qh, q_pos, k_all, v_all, k_pos, lam, slopes)
    o = rmsnorm(o, lp['subln_g']) * (1.0 - lambda_init)
    attn_out = jnp.einsum('btc,cd->btd', o.reshape(B, T, ATTN_WIDTH) * jax.nn.silu(az), lp['w_br_attn'])
    mo = memory_attention(mq.reshape(B, T, MEM_HEADS, MEM_HEAD_DIM), mem_k, mem_v)
    mem_out = jnp.einsum('btc,cd->btd', mo.reshape(B, T, MEM_WIDTH), lp['w_br_mem'])
    g = jax.nn.sigmoid(gates.astype(jnp.float32)).astype(x.dtype).reshape(B, T, N_BRANCH, D_MODEL)
    merged = g[:, :, 0] * pool_out + g[:, :, 1] * ssm_out + g[:, :, 2] * attn_out + g[:, :, 3] * mem_out
    y = x + jnp.einsum('btd,de->bte', merged, lp['w_out'])
    return y, pool_hist_new, ssm_re_new, ssm_im_new, kh, vh


def setup_inputs(seed: int = 0) -> dict:
    key = jax.random.key(seed)
    ks = iter(jax.random.split(key, 48))
    nrm = lambda shape, s: jax.random.normal(next(ks), shape, jnp.float32) * s
    L = DEPTH
    inp = {}
    inp['x_prompt'] = nrm((BATCH, SEQ, D_MODEL), 1.0)
    inp['x_sample'] = nrm((DEC_BATCH, DEC_SEQ, D_MODEL), 1.0)
    inp['mem_prompt'] = nrm((BATCH, N_MEM, D_MODEL), 1.0)
    inp['cache_attn_k'] = nrm((L, DEC_BATCH, PAST_LEN, ATTN_HEADS, ATTN_V_DIM), 1.0)
    inp['cache_attn_v'] = nrm((L, DEC_BATCH, PAST_LEN, ATTN_HEADS, ATTN_V_DIM), 1.0)
    inp['cache_mem_k'] = nrm((L, DEC_BATCH, N_MEM, MEM_HEADS, MEM_HEAD_DIM), 1.0)
    inp['cache_mem_v'] = nrm((L, DEC_BATCH, N_MEM, MEM_HEADS, MEM_HEAD_DIM), 1.0)
    inp['state_pool'] = nrm((L, DEC_BATCH, POOL_HIST, POOL_WIDTH), 1.0)
    inp['state_ssm_re'] = nrm((L, DEC_BATCH, SSM_GROUPS, SSM_STATE), 0.5)
    inp['state_ssm_im'] = nrm((L, DEC_BATCH, SSM_GROUPS, SSM_STATE), 0.5)
    inp['norm_g'] = 1.0 + nrm((L, D_MODEL), 0.02)
    inp['w_in'] = nrm((L, D_MODEL, IN_WIDTH), D_MODEL ** -0.5)
    inp['pool_w'] = nrm((L, POOL_GROUPS, POOL_GROUP, POOL_GROUP), POOL_GROUP ** -0.5)
    inp['pool_scale'] = 1.0 + nrm((L, POOL_WIDTH), 0.02)
    inp['ssm_lambda_re'] = -0.5 + nrm((L, SSM_GROUPS, SSM_STATE), 0.01)
    inp['ssm_lambda_im'] = math.pi * jnp.arange(SSM_STATE, dtype=jnp.float32) + nrm((L, SSM_GROUPS, SSM_STATE), 0.01)
    inp['ssm_log_dt'] = jax.random.uniform(next(ks), (L, SSM_GROUPS), jnp.float32, math.log(1e-3), math.log(1e-1))
    inp['ssm_b_re'] = nrm((L, SSM_GROUPS, SSM_STATE, SSM_GROUP), SSM_GROUP ** -0.5)
    inp['ssm_b_im'] = nrm((L, SSM_GROUPS, SSM_STATE, SSM_GROUP), SSM_GROUP ** -0.5)
    inp['ssm_c_re'] = nrm((L, SSM_GROUPS, SSM_GROUP, SSM_STATE), SSM_STATE ** -0.5)
    inp['ssm_c_im'] = nrm((L, SSM_GROUPS, SSM_GROUP, SSM_STATE), SSM_STATE ** -0.5)
    inp['ssm_d'] = 1.0 + nrm((L, SSM_WIDTH), 0.02)
    inp['ssm_w_glu'] = nrm((L, SSM_WIDTH, 2 * SSM_WIDTH), SSM_WIDTH ** -0.5)
    inp['attn_lq1'] = nrm((L, ATTN_HEAD_DIM), 0.1)
    inp['attn_lk1'] = nrm((L, ATTN_HEAD_DIM), 0.1)
    inp['attn_lq2'] = nrm((L, ATTN_HEAD_DIM), 0.1)
    inp['attn_lk2'] = nrm((L, ATTN_HEAD_DIM), 0.1)
    inp['attn_subln_g'] = 1.0 + nrm((L, ATTN_V_DIM), 0.02)
    inp['mem_norm_g'] = 1.0 + nrm((L, D_MODEL), 0.02)
    inp['w_mem_kv'] = nrm((L, D_MODEL, 2 * MEM_WIDTH), D_MODEL ** -0.5)
    inp['w_br_pool'] = nrm((L, POOL_WIDTH, D_MODEL), POOL_WIDTH ** -0.5)
    inp['w_br_ssm'] = nrm((L, SSM_WIDTH, D_MODEL), SSM_WIDTH ** -0.5)
    inp['w_br_attn'] = nrm((L, ATTN_WIDTH, D_MODEL), ATTN_WIDTH ** -0.5)
    inp['w_br_mem'] = nrm((L, MEM_WIDTH, D_MODEL), MEM_WIDTH ** -0.5)
    inp['w_out'] = nrm((L, D_MODEL, D_MODEL), D_MODEL ** -0.5)
    inp['final_norm_g'] = 1.0 + nrm((D_MODEL,), 0.02)
    return inp


def reference(x_prompt, x_sample, mem_prompt, cache_attn_k, cache_attn_v, cache_mem_k, cache_mem_v,
              state_pool, state_ssm_re, state_ssm_im,
              norm_g, w_in, pool_w, pool_scale, ssm_lambda_re, ssm_lambda_im, ssm_log_dt,
              ssm_b_re, ssm_b_im, ssm_c_re, ssm_c_im, ssm_d, ssm_w_glu,
              attn_lq1, attn_lk1, attn_lq2, attn_lk2, attn_subln_g,
              mem_norm_g, w_mem_kv, w_br_pool, w_br_ssm, w_br_attn, w_br_mem, w_out, final_norm_g):
    B = x_prompt.shape[0]
    xp, xs = x_prompt, x_sample
    kp_l, vp_l, mkp_l, mvp_l, poolp_l, srep_l, simp_l = [], [], [], [], [], [], []
    ks_l, vs_l, pools_l, sres_l, sims_l = [], [], [], [], []
    for l in range(DEPTH):
        lp = {'norm_g': norm_g[l], 'w_in': w_in[l], 'pool_w': pool_w[l], 'pool_scale': pool_scale[l],
              'lam_re': ssm_lambda_re[l], 'lam_im': ssm_lambda_im[l], 'log_dt': ssm_log_dt[l],
              'b_re': ssm_b_re[l], 'b_im': ssm_b_im[l], 'c_re': ssm_c_re[l], 'c_im': ssm_c_im[l],
              'ssm_d': ssm_d[l], 'w_glu': ssm_w_glu[l],
              'lq1': attn_lq1[l], 'lk1': attn_lk1[l], 'lq2': attn_lq2[l], 'lk2': attn_lk2[l],
              'subln_g': attn_subln_g[l], 'w_br_pool': w_br_pool[l], 'w_br_ssm': w_br_ssm[l],
              'w_br_attn': w_br_attn[l], 'w_br_mem': w_br_mem[l], 'w_out': w_out[l]}
        lambda_init = 0.8 - 0.6 * math.exp(-0.3 * l)
        mk_p, mv_p = memory_kv(mem_prompt, mem_norm_g[l], w_mem_kv[l])
        zero_pool = jnp.zeros((B, POOL_HIST, POOL_WIDTH), xp.dtype)
        zero_ssm = jnp.zeros((B, SSM_GROUPS, SSM_STATE), jnp.float32)
        empty_kv = jnp.zeros((B, 0, ATTN_HEADS, ATTN_V_DIM), xp.dtype)
        xp, pool_p, sre_p, sim_p, k_p, v_p = trunk_layer(xp, zero_pool, zero_ssm, zero_ssm, empty_kv, empty_kv,
                                                         mk_p, mv_p, lp, lambda_init)
        xs, pool_s, sre_s, sim_s, k_s, v_s = trunk_layer(xs, state_pool[l], state_ssm_re[l], state_ssm_im[l],
                                                         cache_attn_k[l], cache_attn_v[l],
                                                         cache_mem_k[l], cache_mem_v[l], lp, lambda_init)
        kp_l.append(k_p); vp_l.append(v_p); mkp_l.append(mk_p); mvp_l.append(mv_p)
        poolp_l.append(pool_p); srep_l.append(sre_p); simp_l.append(sim_p)
        ks_l.append(k_s); vs_l.append(v_s); pools_l.append(pool_s); sres_l.append(sre_s); sims_l.append(sim_s)
    y_prompt = rmsnorm(xp, final_norm_g)
    y_sample = rmsnorm(xs, final_norm_g)
    attn_k_prompt = jnp.stack(kp_l)
    attn_v_prompt = jnp.stack(vp_l)
    mem_k_prompt = jnp.stack(mkp_l)
    mem_v_prompt = jnp.stack(mvp_l)
    pool_prompt = jnp.stack(poolp_l)
    ssm_re_prompt = jnp.stack(srep_l)
    ssm_im_prompt = jnp.stack(simp_l)
    attn_k_sample = jnp.stack(ks_l)
    attn_v_sample = jnp.stack(vs_l)
    pool_sample = jnp.stack(pools_l)
    ssm_re_sample = jnp.stack(sres_l)
    ssm_im_sample = jnp.stack(sims_l)
    return (y_prompt, y_sample, attn_k_prompt, attn_v_prompt, mem_k_prompt, mem_v_prompt, pool_prompt,
            ssm_re_prompt, ssm_im_prompt, attn_k_sample, attn_v_sample, pool_sample, ssm_re_sample, ssm_im_sample)
```

```python
import functools
import math

import jax
import jax.numpy as jnp
from jax import lax
from jax.experimental import pallas as pl
from jax.experimental.pallas import tpu as pltpu

F32 = jnp.float32
BF16 = jnp.bfloat16

CHUNK = 64
EPS = 1e-6
NEG_INF = -1e30

POOL_WIDTH = 768
POOL_WINDOWS = (2, 4, 8, 16)
POOL_GROUP = POOL_WIDTH // len(POOL_WINDOWS)
POOL_HIST = 15
POOL_PAD = 32

SSM_WIDTH = 768
SSM_GROUP = 16
SSM_GROUPS = SSM_WIDTH // SSM_GROUP
SSM_STATE = 64
SSM_STATES = SSM_GROUPS * SSM_STATE
SSM_LANE_BLOCK = 128
SSM_BLOCKS = SSM_WIDTH // SSM_LANE_BLOCK
SSM_BLOCK_STATES = SSM_STATES // SSM_BLOCKS

ATTN_HEADS = 8
ATTN_HEAD_DIM = 64
ATTN_V_DIM = 128
ATTN_WIDTH = ATTN_HEADS * ATTN_V_DIM

MEM_HEADS = 4
MEM_HEAD_DIM = 64
MEM_WIDTH = MEM_HEADS * MEM_HEAD_DIM

N_BRANCH = 4

VMEM_LIMIT_BYTES = 56 * 1024 * 1024


def _params(*sem):
    return pltpu.CompilerParams(dimension_semantics=sem, vmem_limit_bytes=VMEM_LIMIT_BYTES)


def _const_spec(shape):
    nd = len(shape)
    return pl.BlockSpec(shape, lambda *_: (0,) * nd, pipeline_mode=pl.Buffered(1))


def _rms_rows(x, g):
    return x * lax.rsqrt(jnp.mean(x * x, axis=-1, keepdims=True) + EPS) * g


def _proj_body(x_ref, g_ref, w_ref, *out_refs, segs):
    h = _rms_rows(x_ref[...], g_ref[...]).astype(BF16)
    for (a, b), o_ref in zip(segs, out_refs):
        o_ref[...] = jnp.dot(h, w_ref[:, a:b], preferred_element_type=F32).astype(o_ref.dtype)


def _row_tile(t, want):
    tm = min(t, want)
    assert t % tm == 0
    return tm


def _proj(x, g, w, segs, out_shapes, out_specs, tm):
    B, T, D = x.shape
    return pl.pallas_call(
        functools.partial(_proj_body, segs=segs),
        grid=(B, T // tm),
        in_specs=[pl.BlockSpec((None, tm, D), lambda b, i: (b, i, 0)),
                  _const_spec((1, D)), _const_spec(w.shape)],
        out_specs=out_specs,
        out_shape=out_shapes,
        compiler_params=_params("parallel", "parallel"),
        name="proj",
    )(x, g.reshape(1, D), w)


def _trunk_proj(x, g, w):
    B, T, D = x.shape
    tm = _row_tile(T, 512)
    bounds = [0, 768, 1536, 2304, 3072, 4096, 5120, 6144, 7168, 7424]
    segs = tuple(zip(bounds[:-1], bounds[1:]))
    bt = lambda wd, dt: jax.ShapeDtypeStruct((B, T, wd), dt)
    tb = lambda wd, dt: jax.ShapeDtypeStruct((T, B * wd), dt)
    s_bt = lambda wd: pl.BlockSpec((None, tm, wd), lambda b, i: (b, i, 0))
    s_tb = lambda wd: pl.BlockSpec((tm, wd), lambda b, i: (i, b))
    shapes = [bt(768, F32), bt(768, BF16), tb(768, BF16), tb(768, BF16), bt(1024, BF16),
              bt(1024, F32), bt(1024, F32), bt(1024, BF16), bt(256, BF16)]
    specs = [s_bt(768), s_bt(768), s_tb(768), s_tb(768), s_bt(1024),
             s_bt(1024), s_bt(1024), s_bt(1024), s_bt(256)]
    return _proj(x, g, w, segs, shapes, specs, tm)


def _memory_kv(mem, g, w):
    B, M, D = mem.shape
    segs = ((0, MEM_WIDTH), (MEM_WIDTH, 2 * MEM_WIDTH))
    shapes = [jax.ShapeDtypeStruct((B, M, MEM_WIDTH), F32)] * 2
    specs = [pl.BlockSpec((None, M, MEM_WIDTH), lambda b, i: (b, i, 0))] * 2
    return _proj(mem, g, w, segs, shapes, specs, M)


def _pool_body(u_ref, hist_ref, z_ref, w_ref, scale_ref, out_ref, hist_out_ref,
               ext, b2, b4, b8, *, tt, pos0):
    i = pl.program_id(1)
    n = POOL_PAD + tt

    @pl.when(i == 0)
    def _():
        ext[0:16, :] = jnp.zeros((16, POOL_WIDTH), F32)
        ext[16:32, :] = hist_ref[...]

    @pl.when(i > 0)
    def _():
        ext[16:32, :] = ext[tt + 16:tt + 32, :]

    u = u_ref[...]
    ext[32:n, :] = u
    b2[8:n, :] = ext[8:n, :] + ext[7:n - 1, :]
    b4[16:n, :] = b2[16:n, :] + b2[14:n - 2, :]
    b8[24:n, :] = b4[24:n, :] + b4[20:n - 4, :]
    s2 = b2[32:n, :]
    s4 = b4[32:n, :]
    s8 = b8[32:n, :]
    s16 = s8 + b8[24:n - 8, :]
    col = lax.broadcasted_iota(jnp.int32, (1, POOL_WIDTH), 1)
    wsum = jnp.where(col < POOL_GROUP, s2,
                     jnp.where(col < 2 * POOL_GROUP, s4, jnp.where(col < 3 * POOL_GROUP, s8, s16)))
    win = jnp.where(col < POOL_GROUP, 2, jnp.where(col < 2 * POOL_GROUP, 4, jnp.where(col < 3 * POOL_GROUP, 8, 16)))
    pos = pos0 + i * tt + lax.broadcasted_iota(jnp.int32, (tt, 1), 0)
    cnt = jnp.minimum(win, pos + 1).astype(F32)
    diff = (wsum / cnt - u).astype(BF16)
    y = jnp.dot(diff, w_ref[...], preferred_element_type=F32) * scale_ref[...]
    out_ref[...] = (y * jax.nn.silu(z_ref[...].astype(F32))).astype(out_ref.dtype)
    hist_out_ref[...] = ext[tt + 16:tt + 32, :]


def _pool(pu, hist16, pz, w_bd, scale, pos0):
    B, T, C = pu.shape
    tt = _row_tile(T, 512)
    n = POOL_PAD + tt
    return pl.pallas_call(
        functools.partial(_pool_body, tt=tt, pos0=pos0),
        grid=(B, T // tt),
        in_specs=[pl.BlockSpec((None, tt, C), lambda b, i: (b, i, 0)),
                  pl.BlockSpec((None, 16, C), lambda b, i: (b, 0, 0)),
                  pl.BlockSpec((None, tt, C), lambda b, i: (b, i, 0)),
                  _const_spec((C, C)), _const_spec((1, C))],
        out_specs=[pl.BlockSpec((None, tt, C), lambda b, i: (b, i, 0)),
                   pl.BlockSpec((None, 16, C), lambda b, i: (b, 0, 0))],
        out_shape=[jax.ShapeDtypeStruct((B, T, C), BF16), jax.ShapeDtypeStruct((B, 16, C), F32)],
        scratch_shapes=[pltpu.VMEM((n, C), F32)] * 4,
        compiler_params=_params("parallel", "arbitrary"),
        name="pool",
    )(pu, hist16, pz, w_bd, scale.reshape(1, C))


def _ssm_body(u_ref, z_ref, bw_ref, cw_ref, are_ref, aim_ref, d_ref, wglu_ref, s0re_ref, s0im_ref,
              out_ref, sre_ref, sim_ref, bu, y, *, nb, tt, cw):
    i = pl.program_id(0)

    @pl.when(i == 0)
    def _():
        sre_ref[...] = s0re_ref[...]
        sim_ref[...] = s0im_ref[...]

    half = SSM_BLOCK_STATES
    for j in range(SSM_BLOCKS):
        lanes = slice(j * SSM_LANE_BLOCK, (j + 1) * SSM_LANE_BLOCK)
        bu[...] = jnp.dot(u_ref[:, lanes], bw_ref[j], preferred_element_type=F32)
        for c in range(half // cw):
            st = slice(j * half + c * cw, j * half + (c + 1) * cw)
            cre = slice(c * cw, (c + 1) * cw)
            cim = slice(half + c * cw, half + (c + 1) * cw)
            ar = jnp.broadcast_to(are_ref[:, st], (nb, cw))
            ai = jnp.broadcast_to(aim_ref[:, st], (nb, cw))

            def step(t, carry, cre=cre, cim=cim, ar=ar, ai=ai):
                hr, hi = carry
                rows = pl.ds(pl.multiple_of(t * nb, nb), nb)
                nhr = ar * hr - ai * hi + bu[rows, cre]
                nhi = ar * hi + ai * hr + bu[rows, cim]
                bu[rows, cre] = nhr
                bu[rows, cim] = nhi
                return nhr, nhi

            hr, hi = lax.fori_loop(0, tt, step, (sre_ref[:, st], sim_ref[:, st]))
            sre_ref[:, st] = hr
            sim_ref[:, st] = hi
        y[:, lanes] = jnp.dot(bu[...].astype(BF16), cw_ref[j], preferred_element_type=F32)

    yy = y[...] + d_ref[...] * u_ref[...].astype(F32)
    act = jax.nn.gelu(yy).astype(BF16)
    zz = jnp.dot(act, wglu_ref[...], preferred_element_type=F32)
    o = zz[:, :SSM_WIDTH] * jax.nn.sigmoid(zz[:, SSM_WIDTH:])
    out_ref[...] = (o * jax.nn.silu(z_ref[...].astype(F32))).astype(out_ref.dtype)


def _ssm(su, sz, nb, bw, cw_mat, a_re, a_im, d, w_glu, s0_re, s0_im):
    R, C = su.shape
    T = R // nb
    tt = _row_tile(T, max(8, 1024 // nb))
    rows = tt * nb
    cw = min(SSM_BLOCK_STATES, 128 * max(1, 64 // nb))
    return pl.pallas_call(
        functools.partial(_ssm_body, nb=nb, tt=tt, cw=cw),
        grid=(T // tt,),
        in_specs=[pl.BlockSpec((rows, C), lambda i: (i, 0)),
                  pl.BlockSpec((rows, C), lambda i: (i, 0)),
                  _const_spec(bw.shape), _const_spec(cw_mat.shape),
                  _const_spec((1, SSM_STATES)), _const_spec((1, SSM_STATES)),
                  _const_spec((1, C)), _const_spec(w_glu.shape),
                  _const_spec((nb, SSM_STATES)), _const_spec((nb, SSM_STATES))],
        out_specs=[pl.BlockSpec((rows, C), lambda i: (i, 0)),
                   pl.BlockSpec((nb, SSM_STATES), lambda i: (0, 0)),
                   pl.BlockSpec((nb, SSM_STATES), lambda i: (0, 0))],
        out_shape=[jax.ShapeDtypeStruct((R, C), BF16),
                   jax.ShapeDtypeStruct((nb, SSM_STATES), F32),
                   jax.ShapeDtypeStruct((nb, SSM_STATES), F32)],
        scratch_shapes=[pltpu.VMEM((rows, 2 * SSM_BLOCK_STATES), F32), pltpu.VMEM((rows, C), F32)],
        compiler_params=_params("arbitrary"),
        name="ssm",
    )(su, sz, bw, cw_mat, a_re.reshape(1, -1), a_im.reshape(1, -1), d.reshape(1, C), w_glu, s0_re, s0_im)


def _ssm_weights(lam_re, lam_im, log_dt, b_re, b_im, c_re, c_im):
    dt = jnp.exp(log_dt.astype(F32))[:, None]
    lr, li = lam_re.astype(F32), lam_im.astype(F32)
    mag = jnp.exp(lr * dt)
    ab_re, ab_im = mag * jnp.cos(li * dt), mag * jnp.sin(li * dt)
    nr, ni = ab_re - 1.0, ab_im
    den = lr * lr + li * li
    f_re = (nr * lr + ni * li) / den
    f_im = (ni * lr - nr * li) / den
    br, bi = b_re.astype(F32), b_im.astype(F32)
    bb_re = f_re[..., None] * br - f_im[..., None] * bi
    bb_im = f_re[..., None] * bi + f_im[..., None] * br
    gpb = SSM_GROUPS // SSM_BLOCKS
    eye = jnp.eye(gpb, dtype=F32)

    def in_block(bb):
        bb = bb.reshape(SSM_BLOCKS, gpb, SSM_STATE, SSM_GROUP)
        m = jnp.einsum('jgpc,gh->jgchp', bb, eye)
        return m.reshape(SSM_BLOCKS, gpb * SSM_GROUP, gpb * SSM_STATE)

    def out_block(cc):
        cc = cc.reshape(SSM_BLOCKS, gpb, SSM_GROUP, SSM_STATE)
        m = jnp.einsum('jgcp,gh->jgphc', cc, eye)
        return m.reshape(SSM_BLOCKS, gpb * SSM_STATE, gpb * SSM_GROUP)

    bw = jnp.concatenate([in_block(bb_re), in_block(bb_im)], axis=-1).astype(BF16)
    cwm = jnp.concatenate([out_block(c_re.astype(F32)), -out_block(c_im.astype(F32))], axis=1).astype(BF16)
    return bw, cwm, ab_re.reshape(-1), ab_im.reshape(-1)


def _split_q(q):
    qf = q.astype(F32) * (ATTN_HEAD_DIM ** -0.5)
    lane = lax.broadcasted_iota(jnp.int32, (1, ATTN_V_DIM), 1)
    first = lane < ATTN_HEAD_DIM
    return jnp.where(first, qf, 0.0).astype(BF16), jnp.where(first, 0.0, qf).astype(BF16)


def _scores(q, k):
    return lax.dot_general(q, k, (((1,), (1,)), ((), ())), preferred_element_type=F32)


def _online(t, v, m, l, acc):
    m_new = jnp.maximum(m, jnp.max(t, axis=-1, keepdims=True))
    alpha = jnp.exp(m - m_new)
    p = jnp.exp(t - m_new)
    l_new = alpha * l + jnp.sum(p, axis=-1, keepdims=True)
    acc_new = alpha * acc + jnp.dot(p.astype(BF16), v, preferred_element_type=F32)
    return m_new, l_new, acc_new


def _attn_finish(o1, o2, lq1, lk1, lq2, lk2, subg, az, lambda_init):
    lam = (jnp.exp(jnp.sum(lq1 * lk1, axis=-1, keepdims=True))
           - jnp.exp(jnp.sum(lq2 * lk2, axis=-1, keepdims=True)) + lambda_init)
    o = o1 - lam * o2
    o = _rms_rows(o, subg) * (1.0 - lambda_init)
    return o * jax.nn.silu(az.astype(F32))


def _attn_prompt_body(slope_ref, lq1_ref, lk1_ref, lq2_ref, lk2_ref, subg_ref,
                      q_ref, k_ref, v_ref, az_ref, out_ref, toep, diag, *, tq, lambda_init):
    qi = pl.program_id(2)
    slope = slope_ref[0:1, 0:1]

    @pl.when(qi == 0)
    def _():
        r = lax.broadcasted_iota(jnp.int32, (tq, tq), 0)
        c = lax.broadcasted_iota(jnp.int32, (tq, tq), 1)
        d = (r - c).astype(F32)
        toep[...] = -slope * d
        visible = (c >> 6) <= (r >> 6)
        diag[...] = jnp.where(visible, -slope * jnp.abs(d), NEG_INF)

    qa, qb = _split_q(q_ref[...])

    def kv(ki):
        rows = pl.ds(pl.multiple_of(ki * tq, tq), tq)
        return k_ref[rows, :].astype(BF16), v_ref[rows, :].astype(BF16)

    def block(ki, bias, carry):
        k, v = kv(ki)
        m1, l1, a1, m2, l2, a2 = carry
        m1, l1, a1 = _online(_scores(qa, k) + bias, v, m1, l1, a1)
        m2, l2, a2 = _online(_scores(qb, k) + bias, v, m2, l2, a2)
        return m1, l1, a1, m2, l2, a2

    def off_diag(ki, carry):
        shift = -slope * ((qi - ki) * tq).astype(F32)
        return block(ki, toep[...] + shift, carry)

    zero = jnp.zeros((tq, 1), F32)
    ninf = jnp.full((tq, 1), NEG_INF, F32)
    acc0 = jnp.zeros((tq, ATTN_V_DIM), F32)
    carry = lax.fori_loop(0, qi, off_diag, (ninf, zero, acc0, ninf, zero, acc0))
    m1, l1, a1, m2, l2, a2 = block(qi, diag[...], carry)
    o = _attn_finish(a1 / l1, a2 / l2, lq1_ref[...], lk1_ref[...], lq2_ref[...], lk2_ref[...],
                     subg_ref[...], az_ref[...], lambda_init)
    out_ref[...] = o.astype(out_ref.dtype)


def _attn_small_inputs(slopes, lq1, lk1, lq2, lk2, subg):
    slope_rows = jnp.broadcast_to(slopes[:, None, None], (ATTN_HEADS, 1, 128))
    r = lambda a: a.astype(F32).reshape(1, -1)
    return slope_rows, r(lq1), r(lk1), r(lq2), r(lk2), r(subg)


def _small_specs():
    return [pl.BlockSpec((None, 1, 128), lambda b, h, *_: (h, 0, 0)),
            _const_spec((1, ATTN_HEAD_DIM)), _const_spec((1, ATTN_HEAD_DIM)),
            _const_spec((1, ATTN_HEAD_DIM)), _const_spec((1, ATTN_HEAD_DIM)),
            _const_spec((1, ATTN_V_DIM))]


def _attn_prompt(q, k, v, az, small, lambda_init):
    B, T, W = q.shape
    tq = _row_tile(T, 256)
    assert tq % CHUNK == 0
    hd = ATTN_V_DIM
    tile = pl.BlockSpec((None, tq, hd), lambda b, h, i: (b, i, h))
    full = pl.BlockSpec((None, T, hd), lambda b, h, i: (b, 0, h))
    return pl.pallas_call(
        functools.partial(_attn_prompt_body, tq=tq, lambda_init=lambda_init),
        grid=(B, ATTN_HEADS, T // tq),
        in_specs=_small_specs() + [tile, full, full, tile],
        out_specs=tile,
        out_shape=jax.ShapeDtypeStruct((B, T, W), BF16),
        scratch_shapes=[pltpu.VMEM((tq, tq), F32)] * 2,
        compiler_params=_params("parallel", "parallel", "arbitrary"),
        name="attn_prompt",
    )(*small, q, k, v, az)


def _attn_sample_body(slope_ref, lq1_ref, lk1_ref, lq2_ref, lk2_ref, subg_ref,
                      q_ref, kp_ref, vp_ref, kn_ref, vn_ref, az_ref, out_ref, *, past, lambda_init):
    slope = slope_ref[0:1, 0:1]
    tq = q_ref.shape[0]
    qa, qb = _split_q(q_ref[...])
    qpos = past + lax.broadcasted_iota(jnp.int32, (tq, 1), 0)

    def bias_for(k0, n):
        kpos = k0 + lax.broadcasted_iota(jnp.int32, (1, n), 1)
        dist = jnp.abs(qpos - kpos).astype(F32)
        return jnp.where((kpos >> 6) <= (qpos >> 6), -slope * dist, NEG_INF)

    zero = jnp.zeros((tq, 1), F32)
    ninf = jnp.full((tq, 1), NEG_INF, F32)
    acc0 = jnp.zeros((tq, ATTN_V_DIM), F32)
    carry = (ninf, zero, acc0, ninf, zero, acc0)
    pieces = [(kn_ref, vn_ref, past, tq)]
    if past:
        pieces.insert(0, (kp_ref, vp_ref, 0, past))
    for kr, vr, k0, n in pieces:
        k, v = kr[...].astype(BF16), vr[...].astype(BF16)
        bias = bias_for(k0, n)
        m1, l1, a1, m2, l2, a2 = carry
        m1, l1, a1 = _online(_scores(qa, k) + bias, v, m1, l1, a1)
        m2, l2, a2 = _online(_scores(qb, k) + bias, v, m2, l2, a2)
        carry = (m1, l1, a1, m2, l2, a2)
    m1, l1, a1, m2, l2, a2 = carry
    o = _attn_finish(a1 / l1, a2 / l2, lq1_ref[...], lk1_ref[...], lq2_ref[...], lk2_ref[...],
                     subg_ref[...], az_ref[...], lambda_init)
    out_ref[...] = o.astype(out_ref.dtype)


def _attn_sample(q, k_past, v_past, k, v, az, small, lambda_init):
    B, T, W = q.shape
    P = k_past.shape[1]
    assert P % CHUNK == 0 and T == CHUNK and P > 0
    hd = ATTN_V_DIM
    tile = pl.BlockSpec((None, T, hd), lambda b, h: (b, 0, h))
    pastb = pl.BlockSpec((None, P, hd), lambda b, h: (b, 0, h))
    return pl.pallas_call(
        functools.partial(_attn_sample_body, past=P, lambda_init=lambda_init),
        grid=(B, ATTN_HEADS),
        in_specs=_small_specs() + [tile, pastb, pastb, tile, tile, tile],
        out_specs=tile,
        out_shape=jax.ShapeDtypeStruct((B, T, W), BF16),
        compiler_params=_params("parallel", "parallel"),
        name="attn_sample",
    )(*small, q, k_past.reshape(B, P, W), v_past.reshape(B, P, W), k, v, az)


def _memat_body(q_ref, k_ref, v_ref, out_ref):
    q = q_ref[...]
    k = k_ref[...]
    v = v_ref[...]
    lane = lax.broadcasted_iota(jnp.int32, (1, MEM_WIDTH), 1)
    acc = jnp.zeros(out_ref.shape, F32)
    for h in range(MEM_HEADS):
        mine = (lane >= h * MEM_HEAD_DIM) & (lane < (h + 1) * MEM_HEAD_DIM)
        kh = jnp.where(mine, k, 0.0).astype(BF16)
        vh = jnp.where(mine, v, 0.0).astype(BF16)
        s = _scores(q, kh) * (MEM_HEAD_DIM ** -0.5)
        p = jnp.exp(s - jnp.max(s, axis=-1, keepdims=True))
        a = p / jnp.sum(p, axis=-1, keepdims=True)
        acc = acc + jnp.dot(a.astype(BF16), vh, preferred_element_type=F32)
    out_ref[...] = acc.astype(out_ref.dtype)


def _memat(mq, mk, mv):
    B, T, W = mq.shape
    M = mk.shape[1]
    tm = _row_tile(T, 512)
    return pl.pallas_call(
        _memat_body,
        grid=(B, T // tm),
        in_specs=[pl.BlockSpec((None, tm, W), lambda b, i: (b, i, 0)),
                  pl.BlockSpec((None, M, W), lambda b, i: (b, 0, 0)),
                  pl.BlockSpec((None, M, W), lambda b, i: (b, 0, 0))],
        out_specs=pl.BlockSpec((None, tm, W), lambda b, i: (b, i, 0)),
        out_shape=jax.ShapeDtypeStruct((B, T, W), BF16),
        compiler_params=_params("parallel", "parallel"),
        name="memat",
    )(mq, mk, mv)


def _merge_body(x_ref, g_ref, pool_ref, ssm_ref, attn_ref, mem_ref,
                wp_ref, ws_ref, wa_ref, wm_ref, wg_ref, wo_ref, fg_ref, out_ref, *, final_norm):
    x = x_ref[...]
    D = x.shape[-1]
    h = _rms_rows(x, g_ref[...]).astype(BF16)
    merged = jnp.zeros(x.shape, F32)
    for n, (a_ref, w_ref) in enumerate(((pool_ref, wp_ref), (ssm_ref, ws_ref), (attn_ref, wa_ref), (mem_ref, wm_ref))):
        gate = jax.nn.sigmoid(jnp.dot(h, wg_ref[:, n * D:(n + 1) * D], preferred_element_type=F32))
        merged = merged + gate * jnp.dot(a_ref[...], w_ref[...], preferred_element_type=F32)
    y = x + jnp.dot(merged.astype(BF16), wo_ref[...], preferred_element_type=F32)
    if final_norm:
        y = _rms_rows(y, fg_ref[...])
    out_ref[...] = y


def _merge(x, g, pool_g, ssm_g, attn_g, mem_o, wp, ws, wa, wm, wg, wo, fg, final_norm):
    B, T, D = x.shape
    tm = _row_tile(T, 512)
    bt = lambda wd: pl.BlockSpec((None, tm, wd), lambda b, i: (b, i, 0))
    return pl.pallas_call(
        functools.partial(_merge_body, final_norm=final_norm),
        grid=(B, T // tm),
        in_specs=[bt(D), _const_spec((1, D)), bt(POOL_WIDTH),
                  pl.BlockSpec((tm, SSM_WIDTH), lambda b, i: (i, b)),
                  bt(ATTN_WIDTH), bt(MEM_WIDTH),
                  _const_spec(wp.shape), _const_spec(ws.shape), _const_spec(wa.shape), _const_spec(wm.shape),
                  _const_spec(wg.shape), _const_spec(wo.shape), _const_spec((1, D))],
        out_specs=bt(D),
        out_shape=jax.ShapeDtypeStruct((B, T, D), F32),
        compiler_params=_params("parallel", "parallel"),
        name="merge",
    )(x, g.reshape(1, D), pool_g, ssm_g, attn_g, mem_o, wp, ws, wa, wm, wg, wo, fg.reshape(1, D))


def _trunk_layer(x, pool_hist, ssm_re, ssm_im, k_past, v_past, mem_k, mem_v, lw, lambda_init, final_g):
    B, T, D = x.shape
    pu, pz, su, sz, q, k, v, az, mq = _trunk_proj(x, lw['norm_g'], lw['w_in_main'])
    pos0 = 0 if k_past is None else k_past.shape[1]
    hist16 = jnp.concatenate([jnp.zeros((B, 1, POOL_WIDTH), F32), pool_hist], axis=1)
    pool_g, hist_new = _pool(pu, hist16, pz, lw['pool_w_bd'], lw['pool_scale'], pos0)
    ssm_g, sre, sim = _ssm(su.reshape(T * B, SSM_WIDTH), sz.reshape(T * B, SSM_WIDTH), B,
                           lw['ssm_bw'], lw['ssm_cw'], lw['ssm_a_re'], lw['ssm_a_im'], lw['ssm_d'], lw['w_glu'],
                           ssm_re.reshape(B, SSM_STATES), ssm_im.reshape(B, SSM_STATES))
    if k_past is None:
        attn_g = _attn_prompt(q, k, v, az, lw['attn_small'], lambda_init)
    else:
        attn_g = _attn_sample(q, k_past, v_past, k, v, az, lw['attn_small'], lambda_init)
    mem_o = _memat(mq, mem_k, mem_v)
    y = _merge(x, lw['norm_g'], pool_g, ssm_g.reshape(T, B * SSM_WIDTH), attn_g, mem_o,
               lw['w_br_pool'], lw['w_br_ssm'], lw['w_br_attn'], lw['w_br_mem'], lw['w_in_gates'], lw['w_out'],
               final_g, final_g is not None and lw['is_last'])
    shape_kv = (B, T, ATTN_HEADS, ATTN_V_DIM)
    return (y, hist_new[:, 1:], sre.reshape(B, SSM_GROUPS, SSM_STATE), sim.reshape(B, SSM_GROUPS, SSM_STATE),
            k.reshape(shape_kv), v.reshape(shape_kv))


def kernel(x_prompt, x_sample, mem_prompt, cache_attn_k, cache_attn_v, cache_mem_k, cache_mem_v, state_pool, state_ssm_re, state_ssm_im, norm_g, w_in, pool_w, pool_scale, ssm_lambda_re, ssm_lambda_im, ssm_log_dt, ssm_b_re, ssm_b_im, ssm_c_re, ssm_c_im, ssm_d, ssm_w_glu, attn_lq1, attn_lk1, attn_lq2, attn_lk2, attn_subln_g, mem_norm_g, w_mem_kv, w_br_pool, w_br_ssm, w_br_attn, w_br_mem, w_out, final_norm_g):
    depth = w_in.shape[0]
    B = x_prompt.shape[0]
    Bd = x_sample.shape[0]
    n_main = w_in.shape[-1] - N_BRANCH * x_prompt.shape[-1]
    slopes = jnp.asarray([2.0 ** (-8.0 * (h + 1) / ATTN_HEADS) for h in range(ATTN_HEADS)], F32)
    xp, xs = x_prompt, x_sample
    outs = {n: [] for n in ('kp', 'vp', 'mkp', 'mvp', 'poolp', 'srep', 'simp', 'ks', 'vs', 'pools', 'sres', 'sims')}
    for l in range(depth):
        bw, cwm, a_re, a_im = _ssm_weights(ssm_lambda_re[l], ssm_lambda_im[l], ssm_log_dt[l],
                                           ssm_b_re[l], ssm_b_im[l], ssm_c_re[l], ssm_c_im[l])
        eye = jnp.eye(len(POOL_WINDOWS), dtype=F32)
        pool_w_bd = jnp.einsum('gcd,gh->gchd', pool_w[l].astype(F32), eye).reshape(POOL_WIDTH, POOL_WIDTH)
        lw = {
            'norm_g': norm_g[l], 'w_in_main': w_in[l, :, :n_main].astype(BF16),
            'w_in_gates': w_in[l, :, n_main:].astype(BF16),
            'pool_w_bd': pool_w_bd.astype(BF16), 'pool_scale': pool_scale[l],
            'ssm_bw': bw, 'ssm_cw': cwm, 'ssm_a_re': a_re, 'ssm_a_im': a_im, 'ssm_d': ssm_d[l],
            'w_glu': ssm_w_glu[l].astype(BF16),
            'attn_small': _attn_small_inputs(slopes, attn_lq1[l], attn_lk1[l], attn_lq2[l], attn_lk2[l],
                                             attn_subln_g[l]),
            'w_br_pool': w_br_pool[l].astype(BF16), 'w_br_ssm': w_br_ssm[l].astype(BF16),
            'w_br_attn': w_br_attn[l].astype(BF16), 'w_br_mem': w_br_mem[l].astype(BF16),
            'w_out': w_out[l].astype(BF16), 'is_last': l == depth - 1,
        }
        lambda_init = 0.8 - 0.6 * math.exp(-0.3 * l)
        mk_p, mv_p = _memory_kv(mem_prompt, mem_norm_g[l], w_mem_kv[l].astype(BF16))
        zero_pool = jnp.zeros((B, POOL_HIST, POOL_WIDTH), F32)
        zero_ssm = jnp.zeros((B, SSM_GROUPS, SSM_STATE), F32)
        xp, pool_p, sre_p, sim_p, k_p, v_p = _trunk_layer(
            xp, zero_pool, zero_ssm, zero_ssm, None, None, mk_p, mv_p, lw, lambda_init, final_norm_g)
        M = cache_mem_k.shape[2]
        xs, pool_s, sre_s, sim_s, k_s, v_s = _trunk_layer(
            xs, state_pool[l], state_ssm_re[l], state_ssm_im[l], cache_attn_k[l], cache_attn_v[l],
            cache_mem_k[l].reshape(Bd, M, MEM_WIDTH), cache_mem_v[l].reshape(Bd, M, MEM_WIDTH),
            lw, lambda_init, final_norm_g)
        mshape = (B, mem_prompt.shape[1], MEM_HEADS, MEM_HEAD_DIM)
        for n, a in (('kp', k_p), ('vp', v_p), ('mkp', mk_p.reshape(mshape)), ('mvp', mv_p.reshape(mshape)),
                     ('poolp', pool_p), ('srep', sre_p), ('simp', sim_p),
                     ('ks', k_s), ('vs', v_s), ('pools', pool_s), ('sres', sre_s), ('sims', sim_s)):
            outs[n].append(a)
    st = {n: jnp.stack(a) for n, a in outs.items()}
    return (xp, xs, st['kp'], st['vp'], st['mkp'], st['mvp'], st['poolp'], st['srep'], st['simp'],
            st['ks'], st['vs'], st['pools'], st['sres'], st['sims'])
```

```python
import functools
import math

import jax
import jax.numpy as jnp
from jax import lax
from jax.experimental import pallas as pl
from jax.experimental.pallas import tpu as pltpu

F32 = jnp.float32
BF16 = jnp.bfloat16

CHUNK = 64
EPS = 1e-6
NEG_INF = -1e30

POOL_WIDTH = 768
POOL_WINDOWS = (2, 4, 8, 16)
POOL_GROUP = POOL_WIDTH // len(POOL_WINDOWS)
POOL_HIST = 15
POOL_PAD = 32

SSM_WIDTH = 768
SSM_GROUP = 16
SSM_GROUPS = SSM_WIDTH // SSM_GROUP
SSM_STATE = 64
SSM_STATES = SSM_GROUPS * SSM_STATE
SSM_LANE_BLOCK = 128
SSM_BLOCKS = SSM_WIDTH // SSM_LANE_BLOCK
SSM_BLOCK_STATES = SSM_STATES // SSM_BLOCKS

ATTN_HEADS = 8
ATTN_HEAD_DIM = 64
ATTN_V_DIM = 128
ATTN_WIDTH = ATTN_HEADS * ATTN_V_DIM

MEM_HEADS = 4
MEM_HEAD_DIM = 64
MEM_WIDTH = MEM_HEADS * MEM_HEAD_DIM

N_BRANCH = 4

VMEM_LIMIT_BYTES = 56 * 1024 * 1024


def _params(*sem):
    return pltpu.CompilerParams(dimension_semantics=sem, vmem_limit_bytes=VMEM_LIMIT_BYTES)


def _const_spec(shape):
    nd = len(shape)
    return pl.BlockSpec(shape, lambda *_: (0,) * nd, pipeline_mode=pl.Buffered(1))


def _rms_rows(x, g):
    return x * lax.rsqrt(jnp.mean(x * x, axis=-1, keepdims=True) + EPS) * g


def _proj_body(x_ref, g_ref, w_ref, *out_refs, segs):
    h = _rms_rows(x_ref[...], g_ref[...]).astype(BF16)
    outs = iter(out_refs)
    for a, b, copies in segs:
        y = jnp.dot(h, w_ref[:, a:b], preferred_element_type=F32)
        for _ in range(copies):
            o_ref = next(outs)
            o_ref[...] = y.astype(o_ref.dtype)


def _row_tile(t, want):
    tm = min(t, want)
    assert t % tm == 0
    return tm


def _proj(x, g, w, segs, out_shapes, out_specs, tm, carried=None):
    B, T, D = x.shape
    carried = carried or {}

    def body(x_ref, g_ref, w_ref, *refs):
        _proj_body(x_ref, g_ref, w_ref, *refs[len(carried):], segs=segs)

    return pl.pallas_call(
        body,
        grid=(B, T // tm),
        in_specs=[pl.BlockSpec((None, tm, D), lambda b, i: (b, i, 0)),
                  _const_spec((1, D)), _const_spec(w.shape)]
        + [pl.BlockSpec(memory_space=pl.ANY)] * len(carried),
        out_specs=out_specs,
        out_shape=out_shapes,
        input_output_aliases={3 + n: o for n, o in enumerate(carried)},
        compiler_params=_params("parallel", "parallel"),
        name="proj",
    )(x, g.reshape(1, D), w, *carried.values())


def _trunk_proj(x, g, w, layer, depth, kv_stacks):
    B, T, D = x.shape
    tm = _row_tile(T, 512)
    bounds = [0, 768, 1536, 2304, 3072, 4096, 5120, 6144, 7168, 7424]
    copies = [1, 1, 1, 1, 1, 2, 2, 1, 1]
    segs = tuple(zip(bounds[:-1], bounds[1:], copies))
    bt = lambda wd, dt: jax.ShapeDtypeStruct((B, T, wd), dt)
    tb = lambda wd, dt: jax.ShapeDtypeStruct((T, B * wd), dt)
    s_bt = lambda wd: pl.BlockSpec((None, tm, wd), lambda b, i: (b, i, 0))
    s_tb = lambda wd: pl.BlockSpec((tm, wd), lambda b, i: (i, b))
    stack = jax.ShapeDtypeStruct((depth, B, T, ATTN_WIDTH), F32)
    s_stack = pl.BlockSpec((None, None, tm, ATTN_WIDTH), lambda b, i: (layer, b, i, 0))
    shapes = [bt(768, F32), bt(768, BF16), tb(768, BF16), tb(768, BF16), bt(1024, BF16),
              stack, bt(1024, BF16), stack, bt(1024, BF16), bt(1024, BF16), bt(256, BF16)]
    specs = [s_bt(768), s_bt(768), s_tb(768), s_tb(768), s_bt(1024),
             s_stack, s_bt(1024), s_stack, s_bt(1024), s_bt(1024), s_bt(256)]
    carried = None if kv_stacks is None else {5: kv_stacks[0], 7: kv_stacks[1]}
    return _proj(x, g, w, segs, shapes, specs, tm, carried)


def _memory_kv(mem, g, w):
    B, M, D = mem.shape
    segs = ((0, MEM_WIDTH, 1), (MEM_WIDTH, 2 * MEM_WIDTH, 1))
    shapes = [jax.ShapeDtypeStruct((B, M, MEM_WIDTH), F32)] * 2
    specs = [pl.BlockSpec((None, M, MEM_WIDTH), lambda b, i: (b, i, 0))] * 2
    return _proj(mem, g, w, segs, shapes, specs, M)


def _pool_body(u_ref, hist_ref, z_ref, w_ref, scale_ref, out_ref, hist_out_ref,
               ext, b2, b4, b8, *, tt, pos0):
    i = pl.program_id(1)
    n = POOL_PAD + tt

    @pl.when(i == 0)
    def _():
        ext[0:16, :] = jnp.zeros((16, POOL_WIDTH), F32)
        ext[16:32, :] = hist_ref[...]

    @pl.when(i > 0)
    def _():
        ext[16:32, :] = ext[tt + 16:tt + 32, :]

    u = u_ref[...]
    ext[32:n, :] = u
    b2[8:n, :] = ext[8:n, :] + ext[7:n - 1, :]
    b4[16:n, :] = b2[16:n, :] + b2[14:n - 2, :]
    b8[24:n, :] = b4[24:n, :] + b4[20:n - 4, :]
    s2 = b2[32:n, :]
    s4 = b4[32:n, :]
    s8 = b8[32:n, :]
    s16 = s8 + b8[24:n - 8, :]
    col = lax.broadcasted_iota(jnp.int32, (1, POOL_WIDTH), 1)
    wsum = jnp.where(col < POOL_GROUP, s2,
                     jnp.where(col < 2 * POOL_GROUP, s4, jnp.where(col < 3 * POOL_GROUP, s8, s16)))
    win = jnp.where(col < POOL_GROUP, 2, jnp.where(col < 2 * POOL_GROUP, 4, jnp.where(col < 3 * POOL_GROUP, 8, 16)))
    pos = pos0 + i * tt + lax.broadcasted_iota(jnp.int32, (tt, 1), 0)
    cnt = jnp.minimum(win, pos + 1).astype(F32)
    diff = (wsum / cnt - u).astype(BF16)
    y = jnp.dot(diff, w_ref[...], preferred_element_type=F32) * scale_ref[...]
    out_ref[...] = (y * jax.nn.silu(z_ref[...].astype(F32))).astype(out_ref.dtype)
    hist_out_ref[...] = ext[tt + 16:tt + 32, :]


def _pool(pu, hist16, pz, w_bd, scale, pos0):
    B, T, C = pu.shape
    tt = _row_tile(T, 512)
    n = POOL_PAD + tt
    return pl.pallas_call(
        functools.partial(_pool_body, tt=tt, pos0=pos0),
        grid=(B, T // tt),
        in_specs=[pl.BlockSpec((None, tt, C), lambda b, i: (b, i, 0)),
                  pl.BlockSpec((None, 16, C), lambda b, i: (b, 0, 0)),
                  pl.BlockSpec((None, tt, C), lambda b, i: (b, i, 0)),
                  _const_spec((C, C)), _const_spec((1, C))],
        out_specs=[pl.BlockSpec((None, tt, C), lambda b, i: (b, i, 0)),
                   pl.BlockSpec((None, 16, C), lambda b, i: (b, 0, 0))],
        out_shape=[jax.ShapeDtypeStruct((B, T, C), BF16), jax.ShapeDtypeStruct((B, 16, C), F32)],
        scratch_shapes=[pltpu.VMEM((n, C), F32)] * 4,
        compiler_params=_params("parallel", "arbitrary"),
        name="pool",
    )(pu, hist16, pz, w_bd, scale.reshape(1, C))


def _ssm_body(u_ref, z_ref, bw_ref, cw_ref, are_ref, aim_ref, d_ref, wglu_ref, s0re_ref, s0im_ref,
              out_ref, sre_ref, sim_ref, bu, y, *, nb, tt, cw):
    i = pl.program_id(0)

    @pl.when(i == 0)
    def _():
        sre_ref[...] = s0re_ref[...]
        sim_ref[...] = s0im_ref[...]

    half = SSM_BLOCK_STATES
    for j in range(SSM_BLOCKS):
        lanes = slice(j * SSM_LANE_BLOCK, (j + 1) * SSM_LANE_BLOCK)
        bu[...] = jnp.dot(u_ref[:, lanes], bw_ref[j], preferred_element_type=F32)
        for c in range(half // cw):
            st = slice(j * half + c * cw, j * half + (c + 1) * cw)
            cre = slice(c * cw, (c + 1) * cw)
            cim = slice(half + c * cw, half + (c + 1) * cw)
            ar = jnp.broadcast_to(are_ref[:, st], (nb, cw))
            ai = jnp.broadcast_to(aim_ref[:, st], (nb, cw))

            def step(t, carry, cre=cre, cim=cim, ar=ar, ai=ai):
                hr, hi = carry
                rows = pl.ds(pl.multiple_of(t * nb, nb), nb)
                nhr = ar * hr - ai * hi + bu[rows, cre]
                nhi = ar * hi + ai * hr + bu[rows, cim]
                bu[rows, cre] = nhr
                bu[rows, cim] = nhi
                return nhr, nhi

            hr, hi = lax.fori_loop(0, tt, step, (sre_ref[:, st], sim_ref[:, st]))
            sre_ref[:, st] = hr
            sim_ref[:, st] = hi
        y[:, lanes] = jnp.dot(bu[...].astype(BF16), cw_ref[j], preferred_element_type=F32)

    yy = y[...] + d_ref[...] * u_ref[...].astype(F32)
    act = jax.nn.gelu(yy).astype(BF16)
    zz = jnp.dot(act, wglu_ref[...], preferred_element_type=F32)
    o = zz[:, :SSM_WIDTH] * jax.nn.sigmoid(zz[:, SSM_WIDTH:])
    out_ref[...] = (o * jax.nn.silu(z_ref[...].astype(F32))).astype(out_ref.dtype)


def _ssm(su, sz, nb, bw, cw_mat, a_re, a_im, d, w_glu, s0_re, s0_im):
    R, C = su.shape
    T = R // nb
    tt = _row_tile(T, max(8, 1024 // nb))
    rows = tt * nb
    cw = min(SSM_BLOCK_STATES, 128 * max(1, 64 // nb))
    return pl.pallas_call(
        functools.partial(_ssm_body, nb=nb, tt=tt, cw=cw),
        grid=(T // tt,),
        in_specs=[pl.BlockSpec((rows, C), lambda i: (i, 0)),
                  pl.BlockSpec((rows, C), lambda i: (i, 0)),
                  _const_spec(bw.shape), _const_spec(cw_mat.shape),
                  _const_spec((1, SSM_STATES)), _const_spec((1, SSM_STATES)),
                  _const_spec((1, C)), _const_spec(w_glu.shape),
                  _const_spec((nb, SSM_STATES)), _const_spec((nb, SSM_STATES))],
        out_specs=[pl.BlockSpec((rows, C), lambda i: (i, 0)),
                   pl.BlockSpec((nb, SSM_STATES), lambda i: (0, 0)),
                   pl.BlockSpec((nb, SSM_STATES), lambda i: (0, 0))],
        out_shape=[jax.ShapeDtypeStruct((R, C), BF16),
                   jax.ShapeDtypeStruct((nb, SSM_STATES), F32),
                   jax.ShapeDtypeStruct((nb, SSM_STATES), F32)],
        scratch_shapes=[pltpu.VMEM((rows, 2 * SSM_BLOCK_STATES), F32), pltpu.VMEM((rows, C), F32)],
        compiler_params=_params("arbitrary"),
        name="ssm",
    )(su, sz, bw, cw_mat, a_re.reshape(1, -1), a_im.reshape(1, -1), d.reshape(1, C), w_glu, s0_re, s0_im)


def _ssm_weights(lam_re, lam_im, log_dt, b_re, b_im, c_re, c_im):
    dt = jnp.exp(log_dt.astype(F32))[:, None]
    lr, li = lam_re.astype(F32), lam_im.astype(F32)
    mag = jnp.exp(lr * dt)
    ab_re, ab_im = mag * jnp.cos(li * dt), mag * jnp.sin(li * dt)
    nr, ni = ab_re - 1.0, ab_im
    den = lr * lr + li * li
    f_re = (nr * lr + ni * li) / den
    f_im = (ni * lr - nr * li) / den
    br, bi = b_re.astype(F32), b_im.astype(F32)
    bb_re = f_re[..., None] * br - f_im[..., None] * bi
    bb_im = f_re[..., None] * bi + f_im[..., None] * br
    gpb = SSM_GROUPS // SSM_BLOCKS
    eye = jnp.eye(gpb, dtype=F32)

    def in_block(bb):
        bb = bb.reshape(SSM_BLOCKS, gpb, SSM_STATE, SSM_GROUP)
        m = jnp.einsum('jgpc,gh->jgchp', bb, eye)
        return m.reshape(SSM_BLOCKS, gpb * SSM_GROUP, gpb * SSM_STATE)

    def out_block(cc):
        cc = cc.reshape(SSM_BLOCKS, gpb, SSM_GROUP, SSM_STATE)
        m = jnp.einsum('jgcp,gh->jgphc', cc, eye)
        return m.reshape(SSM_BLOCKS, gpb * SSM_STATE, gpb * SSM_GROUP)

    bw = jnp.concatenate([in_block(bb_re), in_block(bb_im)], axis=-1).astype(BF16)
    cwm = jnp.concatenate([out_block(c_re.astype(F32)), -out_block(c_im.astype(F32))], axis=1).astype(BF16)
    return bw, cwm, ab_re.reshape(-1), ab_im.reshape(-1)


def _split_q(q):
    qf = q.astype(F32) * (ATTN_HEAD_DIM ** -0.5)
    lane = lax.broadcasted_iota(jnp.int32, (1, ATTN_V_DIM), 1)
    first = lane < ATTN_HEAD_DIM
    return jnp.where(first, qf, 0.0).astype(BF16), jnp.where(first, 0.0, qf).astype(BF16)


def _scores(q, k):
    return lax.dot_general(q, k, (((1,), (1,)), ((), ())), preferred_element_type=F32)


def _online(t, v, m, l, acc):
    m_new = jnp.maximum(m, jnp.max(t, axis=-1, keepdims=True))
    alpha = jnp.exp(m - m_new)
    p = jnp.exp(t - m_new)
    l_new = alpha * l + jnp.sum(p, axis=-1, keepdims=True)
    acc_new = alpha * acc + jnp.dot(p.astype(BF16), v, preferred_element_type=F32)
    return m_new, l_new, acc_new


def _attn_finish(o1, o2, lq1, lk1, lq2, lk2, subg, az, lambda_init):
    lam = (jnp.exp(jnp.sum(lq1 * lk1, axis=-1, keepdims=True))
           - jnp.exp(jnp.sum(lq2 * lk2, axis=-1, keepdims=True)) + lambda_init)
    o = o1 - lam * o2
    o = _rms_rows(o, subg) * (1.0 - lambda_init)
    return o * jax.nn.silu(az.astype(F32))


ATTN_ONES_ROWS = 16
ATTN_HEAD_GROUP = 4


def _online_t(t, vx, m, acc, shift):
    m_new = jnp.maximum(m, jnp.max(t, axis=0, keepdims=True) + shift)
    alpha = jnp.exp(m - m_new)
    p = jnp.exp(t + (shift - m_new))
    return m_new, alpha * acc + jnp.dot(vx, p.astype(BF16), preferred_element_type=F32)


def _attn_prompt_body(slope_ref, lq1_ref, lk1_ref, lq2_ref, lk2_ref, subg_ref,
                      q_ref, k_ref, v_ref, az_ref, out_ref, kx, vxt, corr, wq, *state,
                      tq, nk, heads, lambda_init):
    qi = pl.program_id(2)
    hd = ATTN_V_DIM
    rows_acc = hd + ATTN_ONES_ROWS
    lane = lax.broadcasted_iota(jnp.int32, (tq, hd), 1)
    row = lax.broadcasted_iota(jnp.int32, (tq, hd), 0).astype(F32)
    slopes = [slope_ref[g, 0:1, 0:1] for g in range(heads)]
    cols = [slice(g * hd, (g + 1) * hd) for g in range(heads)]
    mst, ast = state[:2 * heads], state[2 * heads:]

    @pl.when(qi == 0)
    def _():
        ones = jnp.ones((ATTN_ONES_ROWS, tq), BF16)
        c = lax.broadcasted_iota(jnp.int32, (tq, tq), 0)
        r = lax.broadcasted_iota(jnp.int32, (tq, tq), 1)
        visible = (c >> 6) <= (r >> 6)
        ahead = jnp.where(c > r, (c - r).astype(F32), 0.0)
        for g in range(heads):
            kfeat = jnp.where(lane == 0, row, jnp.where(lane == 1, -slopes[g], 0.0)).astype(BF16)
            for j in range(nk):
                rows = slice(j * tq, (j + 1) * tq)
                kx[g, rows, 0:hd] = k_ref[rows, cols[g]]
                kx[g, rows, hd:2 * hd] = kfeat
                vxt[g, j, 0:hd, :] = v_ref[rows, cols[g]].astype(F32).T.astype(BF16)
                vxt[g, j, hd:rows_acc, :] = ones
            corr[g] = jnp.where(visible, -2.0 * slopes[g] * ahead, NEG_INF)

    first = lane < ATTN_HEAD_DIM
    for g in range(heads):
        qf = q_ref[:, cols[g]].astype(F32) * (ATTN_HEAD_DIM ** -0.5)
        qfeat = jnp.where(lane == 0, slopes[g], jnp.where(lane == 1, row, 0.0))
        wq[2 * g] = jnp.concatenate([jnp.where(first, qf, 0.0), qfeat], axis=1).T.astype(BF16)
        wq[2 * g + 1] = jnp.concatenate([jnp.where(first, 0.0, qf), qfeat], axis=1).T.astype(BF16)
    for i in range(2 * heads):
        mst[i][...] = jnp.full((1, tq), NEG_INF, F32)
        ast[i][...] = jnp.zeros((rows_acc, tq), F32)

    def block(ki, diagonal):
        kstart = pl.multiple_of(ki * tq, tq)
        ts = []
        for g in range(heads):
            kb = kx[g, pl.ds(kstart, tq), :]
            for n in range(2):
                t = jnp.dot(kb, wq[2 * g + n], preferred_element_type=F32)
                ts.append(t + corr[g] if diagonal else t)
        for g in range(heads):
            vb = vxt[g, ki]
            shift = jnp.zeros((1, 1), F32) if diagonal else -slopes[g] * ((qi - ki) * tq).astype(F32)
            for n in range(2):
                i = 2 * g + n
                mst[i][...], ast[i][...] = _online_t(ts[i], vb, mst[i][...], ast[i][...], shift)

    def below_diagonal(ki, carry):
        block(ki, False)
        return carry

    lax.fori_loop(0, qi, below_diagonal, 0)
    block(qi, True)

    lam = (jnp.exp(jnp.sum(lq1_ref[...] * lk1_ref[...], axis=-1, keepdims=True))
           - jnp.exp(jnp.sum(lq2_ref[...] * lk2_ref[...], axis=-1, keepdims=True)) + lambda_init)
    for g in range(heads):
        a1, a2 = ast[2 * g][...], ast[2 * g + 1][...]
        o = (a1[0:hd, :] / a1[hd:hd + 1, :] - lam * (a2[0:hd, :] / a2[hd:hd + 1, :])).T
        o = _rms_rows(o, subg_ref[...]) * (1.0 - lambda_init)
        out_ref[:, cols[g]] = (o * jax.nn.silu(az_ref[:, cols[g]].astype(F32))).astype(out_ref.dtype)


def _attn_small_inputs(slopes, lq1, lk1, lq2, lk2, subg):
    slope_rows = jnp.broadcast_to(slopes[:, None, None], (ATTN_HEADS, 1, 128))
    r = lambda a: a.astype(F32).reshape(1, -1)
    return slope_rows, r(lq1), r(lk1), r(lq2), r(lk2), r(subg)


def _small_specs():
    return [pl.BlockSpec((None, 1, 128), lambda b, h, *_: (h, 0, 0)),
            _const_spec((1, ATTN_HEAD_DIM)), _const_spec((1, ATTN_HEAD_DIM)),
            _const_spec((1, ATTN_HEAD_DIM)), _const_spec((1, ATTN_HEAD_DIM)),
            _const_spec((1, ATTN_V_DIM))]


def _attn_prompt(q, k, v, az, small, lambda_init):
    B, T, W = q.shape
    tq = _row_tile(T, 256)
    assert tq % CHUNK == 0 and tq <= 256
    nk = T // tq
    hd = ATTN_V_DIM
    hg = ATTN_HEAD_GROUP
    rows_acc = hd + ATTN_ONES_ROWS
    tile = pl.BlockSpec((None, tq, hg * hd), lambda b, h, i: (b, i, h))
    full = pl.BlockSpec((None, T, hg * hd), lambda b, h, i: (b, 0, h))
    small_specs = [pl.BlockSpec((hg, 1, 128), lambda b, h, i: (h, 0, 0))] + _small_specs()[1:]
    return pl.pallas_call(
        functools.partial(_attn_prompt_body, tq=tq, nk=nk, heads=hg, lambda_init=lambda_init),
        grid=(B, ATTN_HEADS // hg, nk),
        in_specs=small_specs + [tile, full, full, tile],
        out_specs=tile,
        out_shape=jax.ShapeDtypeStruct((B, T, W), BF16),
        scratch_shapes=[pltpu.VMEM((hg, T, 2 * hd), BF16),
                        pltpu.VMEM((hg, nk, rows_acc, tq), BF16),
                        pltpu.VMEM((hg, tq, tq), F32),
                        pltpu.VMEM((2 * hg, 2 * hd, tq), BF16)]
        + [pltpu.VMEM((1, tq), F32)] * (2 * hg) + [pltpu.VMEM((rows_acc, tq), F32)] * (2 * hg),
        compiler_params=_params("parallel", "parallel", "arbitrary"),
        name="attn_prompt",
    )(*small, q, k, v, az)


def _attn_sample_body(slope_ref, lq1_ref, lk1_ref, lq2_ref, lk2_ref, subg_ref,
                      q_ref, kp_ref, vp_ref, kn_ref, vn_ref, az_ref, out_ref, *, past, lambda_init):
    slope = slope_ref[0:1, 0:1]
    tq = q_ref.shape[0]
    qa, qb = _split_q(q_ref[...])
    qpos = past + lax.broadcasted_iota(jnp.int32, (tq, 1), 0)

    def bias_for(k0, n):
        kpos = k0 + lax.broadcasted_iota(jnp.int32, (1, n), 1)
        dist = jnp.abs(qpos - kpos).astype(F32)
        return jnp.where((kpos >> 6) <= (qpos >> 6), -slope * dist, NEG_INF)

    zero = jnp.zeros((tq, 1), F32)
    ninf = jnp.full((tq, 1), NEG_INF, F32)
    acc0 = jnp.zeros((tq, ATTN_V_DIM), F32)
    carry = (ninf, zero, acc0, ninf, zero, acc0)
    pieces = [(kn_ref, vn_ref, past, tq)]
    if past:
        pieces.insert(0, (kp_ref, vp_ref, 0, past))
    for kr, vr, k0, n in pieces:
        k, v = kr[...].astype(BF16), vr[...].astype(BF16)
        bias = bias_for(k0, n)
        m1, l1, a1, m2, l2, a2 = carry
        m1, l1, a1 = _online(_scores(qa, k) + bias, v, m1, l1, a1)
        m2, l2, a2 = _online(_scores(qb, k) + bias, v, m2, l2, a2)
        carry = (m1, l1, a1, m2, l2, a2)
    m1, l1, a1, m2, l2, a2 = carry
    o = _attn_finish(a1 / l1, a2 / l2, lq1_ref[...], lk1_ref[...], lq2_ref[...], lk2_ref[...],
                     subg_ref[...], az_ref[...], lambda_init)
    out_ref[...] = o.astype(out_ref.dtype)


def _attn_sample(q, k_cache, v_cache, k_stack, v_stack, layer, az, small, lambda_init):
    B, T, W = q.shape
    depth, _, P = k_cache.shape[:3]
    assert P % CHUNK == 0 and T == CHUNK and P > 0
    hd = ATTN_V_DIM
    tile = pl.BlockSpec((None, T, hd), lambda b, h: (b, 0, h))
    new = pl.BlockSpec((None, None, T, hd), lambda b, h: (layer, b, 0, h))
    pastb = pl.BlockSpec((None, None, P, hd), lambda b, h: (layer, b, 0, h))
    return pl.pallas_call(
        functools.partial(_attn_sample_body, past=P, lambda_init=lambda_init),
        grid=(B, ATTN_HEADS),
        in_specs=_small_specs() + [tile, pastb, pastb, new, new, tile],
        out_specs=tile,
        out_shape=jax.ShapeDtypeStruct((B, T, W), BF16),
        compiler_params=_params("parallel", "parallel"),
        name="attn_sample",
    )(*small, q, k_cache.reshape(depth, B, P, W), v_cache.reshape(depth, B, P, W), k_stack, v_stack, az)


def _memat_body(q_ref, k_ref, v_ref, out_ref):
    q = q_ref[...]
    k = k_ref[...]
    v = v_ref[...]
    lane = lax.broadcasted_iota(jnp.int32, (1, MEM_WIDTH), 1)
    acc = jnp.zeros(out_ref.shape, F32)
    for h in range(MEM_HEADS):
        mine = (lane >= h * MEM_HEAD_DIM) & (lane < (h + 1) * MEM_HEAD_DIM)
        kh = jnp.where(mine, k, 0.0).astype(BF16)
        vh = jnp.where(mine, v, 0.0).astype(BF16)
        s = _scores(q, kh) * (MEM_HEAD_DIM ** -0.5)
        p = jnp.exp(s - jnp.max(s, axis=-1, keepdims=True))
        a = p / jnp.sum(p, axis=-1, keepdims=True)
        acc = acc + jnp.dot(a.astype(BF16), vh, preferred_element_type=F32)
    out_ref[...] = acc.astype(out_ref.dtype)


def _memat(mq, mk, mv):
    B, T, W = mq.shape
    M = mk.shape[1]
    tm = _row_tile(T, 512)
    return pl.pallas_call(
        _memat_body,
        grid=(B, T // tm),
        in_specs=[pl.BlockSpec((None, tm, W), lambda b, i: (b, i, 0)),
                  pl.BlockSpec((None, M, W), lambda b, i: (b, 0, 0)),
                  pl.BlockSpec((None, M, W), lambda b, i: (b, 0, 0))],
        out_specs=pl.BlockSpec((None, tm, W), lambda b, i: (b, i, 0)),
        out_shape=jax.ShapeDtypeStruct((B, T, W), BF16),
        compiler_params=_params("parallel", "parallel"),
        name="memat",
    )(mq, mk, mv)


def _merge_body(x_ref, g_ref, pool_ref, ssm_ref, attn_ref, mem_ref,
                wp_ref, ws_ref, wa_ref, wm_ref, wg_ref, wo_ref, fg_ref, out_ref, *, final_norm):
    x = x_ref[...]
    D = x.shape[-1]
    h = _rms_rows(x, g_ref[...]).astype(BF16)
    merged = jnp.zeros(x.shape, F32)
    for n, (a_ref, w_ref) in enumerate(((pool_ref, wp_ref), (ssm_ref, ws_ref), (attn_ref, wa_ref), (mem_ref, wm_ref))):
        gate = jax.nn.sigmoid(jnp.dot(h, wg_ref[:, n * D:(n + 1) * D], preferred_element_type=F32))
        merged = merged + gate * jnp.dot(a_ref[...], w_ref[...], preferred_element_type=F32)
    y = x + jnp.dot(merged.astype(BF16), wo_ref[...], preferred_element_type=F32)
    if final_norm:
        y = _rms_rows(y, fg_ref[...])
    out_ref[...] = y


def _merge(x, g, pool_g, ssm_g, attn_g, mem_o, wp, ws, wa, wm, wg, wo, fg, final_norm):
    B, T, D = x.shape
    tm = _row_tile(T, 512)
    bt = lambda wd: pl.BlockSpec((None, tm, wd), lambda b, i: (b, i, 0))
    return pl.pallas_call(
        functools.partial(_merge_body, final_norm=final_norm),
        grid=(B, T // tm),
        in_specs=[bt(D), _const_spec((1, D)), bt(POOL_WIDTH),
                  pl.BlockSpec((tm, SSM_WIDTH), lambda b, i: (i, b)),
                  bt(ATTN_WIDTH), bt(MEM_WIDTH),
                  _const_spec(wp.shape), _const_spec(ws.shape), _const_spec(wa.shape), _const_spec(wm.shape),
                  _const_spec(wg.shape), _const_spec(wo.shape), _const_spec((1, D))],
        out_specs=bt(D),
        out_shape=jax.ShapeDtypeStruct((B, T, D), F32),
        compiler_params=_params("parallel", "parallel"),
        name="merge",
    )(x, g.reshape(1, D), pool_g, ssm_g, attn_g, mem_o, wp, ws, wa, wm, wg, wo, fg.reshape(1, D))


def _trunk_layer(x, layer, depth, pool_hist, ssm_re, ssm_im, caches, kv_stacks, mem_k, mem_v, lw, lambda_init,
                 final_g):
    B, T, D = x.shape
    pu, pz, su, sz, q, k_stack, k16, v_stack, v16, az, mq = _trunk_proj(
        x, lw['norm_g'], lw['w_in_main'], layer, depth, kv_stacks)
    pos0 = 0 if caches is None else caches[0].shape[2]
    hist16 = jnp.concatenate([jnp.zeros((B, 1, POOL_WIDTH), F32), pool_hist], axis=1)
    pool_g, hist_new = _pool(pu, hist16, pz, lw['pool_w_bd'], lw['pool_scale'], pos0)
    ssm_g, sre, sim = _ssm(su.reshape(T * B, SSM_WIDTH), sz.reshape(T * B, SSM_WIDTH), B,
                           lw['ssm_bw'], lw['ssm_cw'], lw['ssm_a_re'], lw['ssm_a_im'], lw['ssm_d'], lw['w_glu'],
                           ssm_re.reshape(B, SSM_STATES), ssm_im.reshape(B, SSM_STATES))
    if caches is None:
        attn_g = _attn_prompt(q, k16, v16, az, lw['attn_small'], lambda_init)
    else:
        attn_g = _attn_sample(q, caches[0], caches[1], k_stack, v_stack, layer, az, lw['attn_small'], lambda_init)
    mem_o = _memat(mq, mem_k, mem_v)
    y = _merge(x, lw['norm_g'], pool_g, ssm_g.reshape(T, B * SSM_WIDTH), attn_g, mem_o,
               lw['w_br_pool'], lw['w_br_ssm'], lw['w_br_attn'], lw['w_br_mem'], lw['w_in_gates'], lw['w_out'],
               final_g, layer == depth - 1)
    return (y, hist_new[:, 1:], sre.reshape(B, SSM_GROUPS, SSM_STATE), sim.reshape(B, SSM_GROUPS, SSM_STATE),
            (k_stack, v_stack))


def kernel(x_prompt, x_sample, mem_prompt, cache_attn_k, cache_attn_v, cache_mem_k, cache_mem_v, state_pool, state_ssm_re, state_ssm_im, norm_g, w_in, pool_w, pool_scale, ssm_lambda_re, ssm_lambda_im, ssm_log_dt, ssm_b_re, ssm_b_im, ssm_c_re, ssm_c_im, ssm_d, ssm_w_glu, attn_lq1, attn_lk1, attn_lq2, attn_lk2, attn_subln_g, mem_norm_g, w_mem_kv, w_br_pool, w_br_ssm, w_br_attn, w_br_mem, w_out, final_norm_g):
    depth = w_in.shape[0]
    B = x_prompt.shape[0]
    Bd = x_sample.shape[0]
    n_main = w_in.shape[-1] - N_BRANCH * x_prompt.shape[-1]
    slopes = jnp.asarray([2.0 ** (-8.0 * (h + 1) / ATTN_HEADS) for h in range(ATTN_HEADS)], F32)
    xp, xs = x_prompt, x_sample
    outs = {n: [] for n in ('mkp', 'mvp', 'poolp', 'srep', 'simp', 'pools', 'sres', 'sims')}
    kv_p = kv_s = None
    for l in range(depth):
        bw, cwm, a_re, a_im = _ssm_weights(ssm_lambda_re[l], ssm_lambda_im[l], ssm_log_dt[l],
                                           ssm_b_re[l], ssm_b_im[l], ssm_c_re[l], ssm_c_im[l])
        eye = jnp.eye(len(POOL_WINDOWS), dtype=F32)
        pool_w_bd = jnp.einsum('gcd,gh->gchd', pool_w[l].astype(F32), eye).reshape(POOL_WIDTH, POOL_WIDTH)
        lw = {
            'norm_g': norm_g[l], 'w_in_main': w_in[l, :, :n_main].astype(BF16),
            'w_in_gates': w_in[l, :, n_main:].astype(BF16),
            'pool_w_bd': pool_w_bd.astype(BF16), 'pool_scale': pool_scale[l],
            'ssm_bw': bw, 'ssm_cw': cwm, 'ssm_a_re': a_re, 'ssm_a_im': a_im, 'ssm_d': ssm_d[l],
            'w_glu': ssm_w_glu[l].astype(BF16),
            'attn_small': _attn_small_inputs(slopes, attn_lq1[l], attn_lk1[l], attn_lq2[l], attn_lk2[l],
                                             attn_subln_g[l]),
            'w_br_pool': w_br_pool[l].astype(BF16), 'w_br_ssm': w_br_ssm[l].astype(BF16),
            'w_br_attn': w_br_attn[l].astype(BF16), 'w_br_mem': w_br_mem[l].astype(BF16),
            'w_out': w_out[l].astype(BF16),
        }
        lambda_init = 0.8 - 0.6 * math.exp(-0.3 * l)
        mk_p, mv_p = _memory_kv(mem_prompt, mem_norm_g[l], w_mem_kv[l].astype(BF16))
        zero_pool = jnp.zeros((B, POOL_HIST, POOL_WIDTH), F32)
        zero_ssm = jnp.zeros((B, SSM_GROUPS, SSM_STATE), F32)
        xp, pool_p, sre_p, sim_p, kv_p = _trunk_layer(
            xp, l, depth, zero_pool, zero_ssm, zero_ssm, None, kv_p, mk_p, mv_p, lw, lambda_init, final_norm_g)
        M = cache_mem_k.shape[2]
        xs, pool_s, sre_s, sim_s, kv_s = _trunk_layer(
            xs, l, depth, state_pool[l], state_ssm_re[l], state_ssm_im[l], (cache_attn_k, cache_attn_v), kv_s,
            cache_mem_k[l].reshape(Bd, M, MEM_WIDTH), cache_mem_v[l].reshape(Bd, M, MEM_WIDTH),
            lw, lambda_init, final_norm_g)
        mshape = (B, mem_prompt.shape[1], MEM_HEADS, MEM_HEAD_DIM)
        for n, a in (('mkp', mk_p.reshape(mshape)), ('mvp', mv_p.reshape(mshape)),
                     ('poolp', pool_p), ('srep', sre_p), ('simp', sim_p),
                     ('pools', pool_s), ('sres', sre_s), ('sims', sim_s)):
            outs[n].append(a)
    st = {n: jnp.stack(a) for n, a in outs.items()}
    heads = lambda a: a.reshape(a.shape[:3] + (ATTN_HEADS, ATTN_V_DIM))
    return (xp, xs, heads(kv_p[0]), heads(kv_p[1]), st['mkp'], st['mvp'], st['poolp'], st['srep'], st['simp'],
            heads(kv_s[0]), heads(kv_s[1]), st['pools'], st['sres'], st['sims'])
```

```python
import functools
import math

import jax
import jax.numpy as jnp
from jax import lax
from jax.experimental import pallas as pl
from jax.experimental.pallas import tpu as pltpu

F32 = jnp.float32
BF16 = jnp.bfloat16

CHUNK = 64
EPS = 1e-6
NEG_INF = -1e30

POOL_WIDTH = 768
POOL_WINDOWS = (2, 4, 8, 16)
POOL_GROUP = POOL_WIDTH // len(POOL_WINDOWS)
POOL_HIST = 15
POOL_PAD = 32

SSM_WIDTH = 768
SSM_GROUP = 16
SSM_GROUPS = SSM_WIDTH // SSM_GROUP
SSM_STATE = 64
SSM_STATES = SSM_GROUPS * SSM_STATE
SSM_LANE_BLOCK = 128
SSM_BLOCKS = SSM_WIDTH // SSM_LANE_BLOCK
SSM_BLOCK_STATES = SSM_STATES // SSM_BLOCKS

ATTN_HEADS = 8
ATTN_HEAD_DIM = 64
ATTN_V_DIM = 128
ATTN_WIDTH = ATTN_HEADS * ATTN_V_DIM

MEM_HEADS = 4
MEM_HEAD_DIM = 64
MEM_WIDTH = MEM_HEADS * MEM_HEAD_DIM

N_BRANCH = 4

VMEM_LIMIT_BYTES = 56 * 1024 * 1024


def _params(*sem):
    return pltpu.CompilerParams(dimension_semantics=sem, vmem_limit_bytes=VMEM_LIMIT_BYTES)


def _const_spec(shape):
    nd = len(shape)
    return pl.BlockSpec(shape, lambda *_: (0,) * nd, pipeline_mode=pl.Buffered(1))


def _rms_rows(x, g):
    return x * lax.rsqrt(jnp.mean(x * x, axis=-1, keepdims=True) + EPS) * g


def _proj_body(x_ref, g_ref, w_ref, *out_refs, segs):
    h = _rms_rows(x_ref[...], g_ref[...]).astype(BF16)
    outs = iter(out_refs)
    for a, b, copies in segs:
        y = jnp.dot(h, w_ref[:, a:b], preferred_element_type=F32)
        for _ in range(copies):
            o_ref = next(outs)
            o_ref[...] = y.astype(o_ref.dtype)


def _row_tile(t, want):
    tm = min(t, want)
    assert t % tm == 0
    return tm


def _proj(x, g, w, segs, out_shapes, out_specs, tm, carried=None):
    B, T, D = x.shape
    carried = carried or {}

    def body(x_ref, g_ref, w_ref, *refs):
        _proj_body(x_ref, g_ref, w_ref, *refs[len(carried):], segs=segs)

    return pl.pallas_call(
        body,
        grid=(B, T // tm),
        in_specs=[pl.BlockSpec((None, tm, D), lambda b, i: (b, i, 0)),
                  _const_spec((1, D)), _const_spec(w.shape)]
        + [pl.BlockSpec(memory_space=pl.ANY)] * len(carried),
        out_specs=out_specs,
        out_shape=out_shapes,
        input_output_aliases={3 + n: o for n, o in enumerate(carried)},
        compiler_params=_params("parallel", "parallel"),
        name="proj",
    )(x, g.reshape(1, D), w, *carried.values())


def _trunk_proj(x, g, w, layer, depth, kv_stacks):
    B, T, D = x.shape
    tm = _row_tile(T, 512)
    bounds = [0, 768, 1536, 2304, 3072, 4096, 5120, 6144, 7168, 7424]
    copies = [1, 1, 1, 1, 1, 2, 2, 1, 1]
    segs = tuple(zip(bounds[:-1], bounds[1:], copies))
    bt = lambda wd, dt: jax.ShapeDtypeStruct((B, T, wd), dt)
    tb = lambda wd, dt: jax.ShapeDtypeStruct((T, B * wd), dt)
    s_bt = lambda wd: pl.BlockSpec((None, tm, wd), lambda b, i: (b, i, 0))
    s_tb = lambda wd: pl.BlockSpec((tm, wd), lambda b, i: (i, b))
    stack = jax.ShapeDtypeStruct((depth, B, T, ATTN_WIDTH), F32)
    s_stack = pl.BlockSpec((None, None, tm, ATTN_WIDTH), lambda b, i: (layer, b, i, 0))
    shapes = [bt(768, F32), bt(768, BF16), tb(768, BF16), tb(768, BF16), bt(1024, BF16),
              stack, bt(1024, BF16), stack, bt(1024, BF16), bt(1024, BF16), bt(256, BF16)]
    specs = [s_bt(768), s_bt(768), s_tb(768), s_tb(768), s_bt(1024),
             s_stack, s_bt(1024), s_stack, s_bt(1024), s_bt(1024), s_bt(256)]
    carried = None if kv_stacks is None else {5: kv_stacks[0], 7: kv_stacks[1]}
    return _proj(x, g, w, segs, shapes, specs, tm, carried)


def _memory_kv(mem, g, w):
    B, M, D = mem.shape
    segs = ((0, MEM_WIDTH, 1), (MEM_WIDTH, 2 * MEM_WIDTH, 1))
    shapes = [jax.ShapeDtypeStruct((B, M, MEM_WIDTH), F32)] * 2
    specs = [pl.BlockSpec((None, M, MEM_WIDTH), lambda b, i: (b, i, 0))] * 2
    return _proj(mem, g, w, segs, shapes, specs, M)


def _pool_body(u_ref, hist_ref, z_ref, w_ref, scale_ref, out_ref, hist_out_ref,
               ext, b2, b4, b8, *, tt, pos0):
    i = pl.program_id(1)
    n = POOL_PAD + tt

    @pl.when(i == 0)
    def _():
        ext[0:16, :] = jnp.zeros((16, POOL_WIDTH), F32)
        ext[16:32, :] = hist_ref[...]

    @pl.when(i > 0)
    def _():
        ext[16:32, :] = ext[tt + 16:tt + 32, :]

    u = u_ref[...]
    ext[32:n, :] = u
    b2[8:n, :] = ext[8:n, :] + ext[7:n - 1, :]
    b4[16:n, :] = b2[16:n, :] + b2[14:n - 2, :]
    b8[24:n, :] = b4[24:n, :] + b4[20:n - 4, :]
    s2 = b2[32:n, :]
    s4 = b4[32:n, :]
    s8 = b8[32:n, :]
    s16 = s8 + b8[24:n - 8, :]
    col = lax.broadcasted_iota(jnp.int32, (1, POOL_WIDTH), 1)
    wsum = jnp.where(col < POOL_GROUP, s2,
                     jnp.where(col < 2 * POOL_GROUP, s4, jnp.where(col < 3 * POOL_GROUP, s8, s16)))
    win = jnp.where(col < POOL_GROUP, 2, jnp.where(col < 2 * POOL_GROUP, 4, jnp.where(col < 3 * POOL_GROUP, 8, 16)))
    pos = pos0 + i * tt + lax.broadcasted_iota(jnp.int32, (tt, 1), 0)
    cnt = jnp.minimum(win, pos + 1).astype(F32)
    diff = (wsum / cnt - u).astype(BF16)
    y = jnp.dot(diff, w_ref[...], preferred_element_type=F32) * scale_ref[...]
    out_ref[...] = (y * jax.nn.silu(z_ref[...].astype(F32))).astype(out_ref.dtype)
    hist_out_ref[...] = ext[tt + 16:tt + 32, :]


def _pool(pu, hist16, pz, w_bd, scale, pos0):
    B, T, C = pu.shape
    tt = _row_tile(T, 512)
    n = POOL_PAD + tt
    return pl.pallas_call(
        functools.partial(_pool_body, tt=tt, pos0=pos0),
        grid=(B, T // tt),
        in_specs=[pl.BlockSpec((None, tt, C), lambda b, i: (b, i, 0)),
                  pl.BlockSpec((None, 16, C), lambda b, i: (b, 0, 0)),
                  pl.BlockSpec((None, tt, C), lambda b, i: (b, i, 0)),
                  _const_spec((C, C)), _const_spec((1, C))],
        out_specs=[pl.BlockSpec((None, tt, C), lambda b, i: (b, i, 0)),
                   pl.BlockSpec((None, 16, C), lambda b, i: (b, 0, 0))],
        out_shape=[jax.ShapeDtypeStruct((B, T, C), BF16), jax.ShapeDtypeStruct((B, 16, C), F32)],
        scratch_shapes=[pltpu.VMEM((n, C), F32)] * 4,
        compiler_params=_params("parallel", "arbitrary"),
        name="pool",
    )(pu, hist16, pz, w_bd, scale.reshape(1, C))


def _ssm_body(u_ref, z_ref, bw_ref, cw_ref, are_ref, aim_ref, d_ref, wglu_ref, s0re_ref, s0im_ref,
              out_ref, sre_ref, sim_ref, bu, y, *, nb, tt, cw):
    i = pl.program_id(0)

    @pl.when(i == 0)
    def _():
        sre_ref[...] = s0re_ref[...]
        sim_ref[...] = s0im_ref[...]

    half = SSM_BLOCK_STATES
    for j in range(SSM_BLOCKS):
        lanes = slice(j * SSM_LANE_BLOCK, (j + 1) * SSM_LANE_BLOCK)
        bu[...] = jnp.dot(u_ref[:, lanes], bw_ref[j], preferred_element_type=F32)
        for c in range(half // cw):
            st = slice(j * half + c * cw, j * half + (c + 1) * cw)
            cre = slice(c * cw, (c + 1) * cw)
            cim = slice(half + c * cw, half + (c + 1) * cw)
            ar = jnp.broadcast_to(are_ref[:, st], (nb, cw))
            ai = jnp.broadcast_to(aim_ref[:, st], (nb, cw))

            def step(t, carry, cre=cre, cim=cim, ar=ar, ai=ai):
                hr, hi = carry
                rows = pl.ds(pl.multiple_of(t * nb, nb), nb)
                nhr = ar * hr - ai * hi + bu[rows, cre]
                nhi = ar * hi + ai * hr + bu[rows, cim]
                bu[rows, cre] = nhr
                bu[rows, cim] = nhi
                return nhr, nhi

            hr, hi = lax.fori_loop(0, tt, step, (sre_ref[:, st], sim_ref[:, st]))
            sre_ref[:, st] = hr
            sim_ref[:, st] = hi
        y[:, lanes] = jnp.dot(bu[...].astype(BF16), cw_ref[j], preferred_element_type=F32)

    yy = y[...] + d_ref[...] * u_ref[...].astype(F32)
    act = jax.nn.gelu(yy).astype(BF16)
    zz = jnp.dot(act, wglu_ref[...], preferred_element_type=F32)
    o = zz[:, :SSM_WIDTH] * jax.nn.sigmoid(zz[:, SSM_WIDTH:])
    out_ref[...] = (o * jax.nn.silu(z_ref[...].astype(F32))).astype(out_ref.dtype)


def _ssm(su, sz, nb, bw, cw_mat, a_re, a_im, d, w_glu, s0_re, s0_im):
    R, C = su.shape
    T = R // nb
    tt = _row_tile(T, max(8, 1024 // nb))
    rows = tt * nb
    cw = min(SSM_BLOCK_STATES, 128 * max(1, 64 // nb))
    return pl.pallas_call(
        functools.partial(_ssm_body, nb=nb, tt=tt, cw=cw),
        grid=(T // tt,),
        in_specs=[pl.BlockSpec((rows, C), lambda i: (i, 0)),
                  pl.BlockSpec((rows, C), lambda i: (i, 0)),
                  _const_spec(bw.shape), _const_spec(cw_mat.shape),
                  _const_spec((1, SSM_STATES)), _const_spec((1, SSM_STATES)),
                  _const_spec((1, C)), _const_spec(w_glu.shape),
                  _const_spec((nb, SSM_STATES)), _const_spec((nb, SSM_STATES))],
        out_specs=[pl.BlockSpec((rows, C), lambda i: (i, 0)),
                   pl.BlockSpec((nb, SSM_STATES), lambda i: (0, 0)),
                   pl.BlockSpec((nb, SSM_STATES), lambda i: (0, 0))],
        out_shape=[jax.ShapeDtypeStruct((R, C), BF16),
                   jax.ShapeDtypeStruct((nb, SSM_STATES), F32),
                   jax.ShapeDtypeStruct((nb, SSM_STATES), F32)],
        scratch_shapes=[pltpu.VMEM((rows, 2 * SSM_BLOCK_STATES), F32), pltpu.VMEM((rows, C), F32)],
        compiler_params=_params("arbitrary"),
        name="ssm",
    )(su, sz, bw, cw_mat, a_re.reshape(1, -1), a_im.reshape(1, -1), d.reshape(1, C), w_glu, s0_re, s0_im)


def _ssm_weights(lam_re, lam_im, log_dt, b_re, b_im, c_re, c_im):
    dt = jnp.exp(log_dt.astype(F32))[:, None]
    lr, li = lam_re.astype(F32), lam_im.astype(F32)
    mag = jnp.exp(lr * dt)
    ab_re, ab_im = mag * jnp.cos(li * dt), mag * jnp.sin(li * dt)
    nr, ni = ab_re - 1.0, ab_im
    den = lr * lr + li * li
    f_re = (nr * lr + ni * li) / den
    f_im = (ni * lr - nr * li) / den
    br, bi = b_re.astype(F32), b_im.astype(F32)
    bb_re = f_re[..., None] * br - f_im[..., None] * bi
    bb_im = f_re[..., None] * bi + f_im[..., None] * br
    gpb = SSM_GROUPS // SSM_BLOCKS
    eye = jnp.eye(gpb, dtype=F32)

    def in_block(bb):
        bb = bb.reshape(SSM_BLOCKS, gpb, SSM_STATE, SSM_GROUP)
        m = jnp.einsum('jgpc,gh->jgchp', bb, eye)
        return m.reshape(SSM_BLOCKS, gpb * SSM_GROUP, gpb * SSM_STATE)

    def out_block(cc):
        cc = cc.reshape(SSM_BLOCKS, gpb, SSM_GROUP, SSM_STATE)
        m = jnp.einsum('jgcp,gh->jgphc', cc, eye)
        return m.reshape(SSM_BLOCKS, gpb * SSM_STATE, gpb * SSM_GROUP)

    bw = jnp.concatenate([in_block(bb_re), in_block(bb_im)], axis=-1).astype(BF16)
    cwm = jnp.concatenate([out_block(c_re.astype(F32)), -out_block(c_im.astype(F32))], axis=1).astype(BF16)
    return bw, cwm, ab_re.reshape(-1), ab_im.reshape(-1)


def _split_q(q):
    qf = q.astype(F32) * (ATTN_HEAD_DIM ** -0.5)
    lane = lax.broadcasted_iota(jnp.int32, (1, ATTN_V_DIM), 1)
    first = lane < ATTN_HEAD_DIM
    return jnp.where(first, qf, 0.0).astype(BF16), jnp.where(first, 0.0, qf).astype(BF16)


def _scores(q, k):
    return lax.dot_general(q, k, (((1,), (1,)), ((), ())), preferred_element_type=F32)


def _online(t, v, m, l, acc):
    m_new = jnp.maximum(m, jnp.max(t, axis=-1, keepdims=True))
    alpha = jnp.exp(m - m_new)
    p = jnp.exp(t - m_new)
    l_new = alpha * l + jnp.sum(p, axis=-1, keepdims=True)
    acc_new = alpha * acc + jnp.dot(p.astype(BF16), v, preferred_element_type=F32)
    return m_new, l_new, acc_new


def _attn_finish(o1, o2, lq1, lk1, lq2, lk2, subg, az, lambda_init):
    lam = (jnp.exp(jnp.sum(lq1 * lk1, axis=-1, keepdims=True))
           - jnp.exp(jnp.sum(lq2 * lk2, axis=-1, keepdims=True)) + lambda_init)
    o = o1 - lam * o2
    o = _rms_rows(o, subg) * (1.0 - lambda_init)
    return o * jax.nn.silu(az.astype(F32))


ATTN_ONES_ROWS = 16
ATTN_HEAD_GROUP = 4


def _online_t(t, vx, m, acc, shift):
    m_new = jnp.maximum(m, jnp.max(t, axis=0, keepdims=True) + shift)
    alpha = jnp.exp(m - m_new)
    p = jnp.exp(t + (shift - m_new))
    return m_new, alpha * acc + jnp.dot(vx, p.astype(BF16), preferred_element_type=F32)


def _attn_prompt_body(slope_ref, lq1_ref, lk1_ref, lq2_ref, lk2_ref, subg_ref,
                      q_ref, k_ref, v_ref, az_ref, out_ref, kx, vxt, corr, wq, *state,
                      tq, nk, heads, lambda_init):
    qi = pl.program_id(2)
    hd = ATTN_V_DIM
    rows_acc = hd + ATTN_ONES_ROWS
    lane = lax.broadcasted_iota(jnp.int32, (tq, hd), 1)
    row = lax.broadcasted_iota(jnp.int32, (tq, hd), 0).astype(F32)
    slopes = [slope_ref[g, 0:1, 0:1] for g in range(heads)]
    cols = [slice(g * hd, (g + 1) * hd) for g in range(heads)]
    nc = 2 * heads
    mst, tmax_a, tmax_b, ast, t_a, t_b = (state[n * nc:(n + 1) * nc] for n in range(6))
    tmax_s, t_s = (tmax_a, tmax_b), (t_a, t_b)

    @pl.when(qi == 0)
    def _():
        ones = jnp.ones((ATTN_ONES_ROWS, tq), BF16)
        c = lax.broadcasted_iota(jnp.int32, (tq, tq), 0)
        r = lax.broadcasted_iota(jnp.int32, (tq, tq), 1)
        visible = (c >> 6) <= (r >> 6)
        ahead = jnp.where(c > r, (c - r).astype(F32), 0.0)
        for g in range(heads):
            kfeat = jnp.where(lane == 0, row, jnp.where(lane == 1, -slopes[g], 0.0)).astype(BF16)
            for j in range(nk):
                rows = slice(j * tq, (j + 1) * tq)
                kx[g, rows, 0:hd] = k_ref[rows, cols[g]]
                kx[g, rows, hd:2 * hd] = kfeat
                vxt[g, j, 0:hd, :] = v_ref[rows, cols[g]].astype(F32).T.astype(BF16)
                vxt[g, j, hd:rows_acc, :] = ones
            corr[g] = jnp.where(visible, -2.0 * slopes[g] * ahead, NEG_INF)

    first = lane < ATTN_HEAD_DIM
    for g in range(heads):
        qf = q_ref[:, cols[g]].astype(F32) * (ATTN_HEAD_DIM ** -0.5)
        qfeat = jnp.where(lane == 0, slopes[g], jnp.where(lane == 1, row, 0.0))
        wq[2 * g] = jnp.concatenate([jnp.where(first, qf, 0.0), qfeat], axis=1).T.astype(BF16)
        wq[2 * g + 1] = jnp.concatenate([jnp.where(first, 0.0, qf), qfeat], axis=1).T.astype(BF16)
    chains = [(g, 2 * g + n) for g in range(heads) for n in range(2)]
    for _, c in chains:
        mst[c][...] = jnp.full((1, tq), NEG_INF, F32)
        ast[c][...] = jnp.zeros((rows_acc, tq), F32)

    def scores(j, buf):
        kstart = pl.multiple_of(j * tq, tq)
        for g, c in chains:
            t = jnp.dot(kx[g, pl.ds(kstart, tq), :], wq[c], preferred_element_type=F32)
            t_s[buf][c][...] = t
            tmax_s[buf][c][...] = jnp.max(t, axis=0, keepdims=True)

    def update(j, buf, diagonal=False):
        for g, c in chains:
            if diagonal:
                t = t_s[buf][c][...] + corr[g]
                tmax = jnp.max(t, axis=0, keepdims=True)
                shift = jnp.zeros((1, 1), F32)
            else:
                t, tmax = t_s[buf][c][...], tmax_s[buf][c][...]
                shift = -slopes[g] * ((qi - j) * tq).astype(F32)
            m_old = mst[c][...]
            m_new = jnp.maximum(m_old, tmax + shift)
            mst[c][...] = m_new
            p = jnp.exp(t + (shift - m_new)).astype(BF16)
            ast[c][...] = jnp.exp(m_old - m_new) * ast[c][...] + jnp.dot(vxt[g, j], p,
                                                                         preferred_element_type=F32)

    def step(j, buf):
        scores(j + 1, 1 - buf)
        update(j, buf)

    def pair(k, carry):
        step(2 * k, 0)
        step(2 * k + 1, 1)
        return carry

    scores(0, 0)
    lax.fori_loop(0, qi // 2, pair, 0)

    @pl.when(qi % 2 == 1)
    def _():
        step(qi - 1, 0)
        update(qi, 1, diagonal=True)

    @pl.when(qi % 2 == 0)
    def _():
        update(qi, 0, diagonal=True)

    lam = (jnp.exp(jnp.sum(lq1_ref[...] * lk1_ref[...], axis=-1, keepdims=True))
           - jnp.exp(jnp.sum(lq2_ref[...] * lk2_ref[...], axis=-1, keepdims=True)) + lambda_init)
    for g in range(heads):
        a1, a2 = ast[2 * g][...], ast[2 * g + 1][...]
        o = (a1[0:hd, :] / a1[hd:hd + 1, :] - lam * (a2[0:hd, :] / a2[hd:hd + 1, :])).T
        o = _rms_rows(o, subg_ref[...]) * (1.0 - lambda_init)
        out_ref[:, cols[g]] = (o * jax.nn.silu(az_ref[:, cols[g]].astype(F32))).astype(out_ref.dtype)


def _attn_small_inputs(slopes, lq1, lk1, lq2, lk2, subg):
    slope_rows = jnp.broadcast_to(slopes[:, None, None], (ATTN_HEADS, 1, 128))
    r = lambda a: a.astype(F32).reshape(1, -1)
    return slope_rows, r(lq1), r(lk1), r(lq2), r(lk2), r(subg)


def _small_specs():
    return [pl.BlockSpec((None, 1, 128), lambda b, h, *_: (h, 0, 0)),
            _const_spec((1, ATTN_HEAD_DIM)), _const_spec((1, ATTN_HEAD_DIM)),
            _const_spec((1, ATTN_HEAD_DIM)), _const_spec((1, ATTN_HEAD_DIM)),
            _const_spec((1, ATTN_V_DIM))]


def _attn_prompt(q, k, v, az, small, lambda_init):
    B, T, W = q.shape
    tq = _row_tile(T, 256)
    assert tq % CHUNK == 0 and tq <= 256
    nk = T // tq
    hd = ATTN_V_DIM
    hg = ATTN_HEAD_GROUP
    rows_acc = hd + ATTN_ONES_ROWS
    tile = pl.BlockSpec((None, tq, hg * hd), lambda b, h, i: (b, i, h))
    full = pl.BlockSpec((None, T, hg * hd), lambda b, h, i: (b, 0, h))
    small_specs = [pl.BlockSpec((hg, 1, 128), lambda b, h, i: (h, 0, 0))] + _small_specs()[1:]
    return pl.pallas_call(
        functools.partial(_attn_prompt_body, tq=tq, nk=nk, heads=hg, lambda_init=lambda_init),
        grid=(B, ATTN_HEADS // hg, nk),
        in_specs=small_specs + [tile, full, full, tile],
        out_specs=tile,
        out_shape=jax.ShapeDtypeStruct((B, T, W), BF16),
        scratch_shapes=[pltpu.VMEM((hg, T, 2 * hd), BF16),
                        pltpu.VMEM((hg, nk, rows_acc, tq), BF16),
                        pltpu.VMEM((hg, tq, tq), F32),
                        pltpu.VMEM((2 * hg, 2 * hd, tq), BF16)]
        + [pltpu.VMEM((1, tq), F32)] * (3 * 2 * hg) + [pltpu.VMEM((rows_acc, tq), F32)] * (2 * hg)
        + [pltpu.VMEM((tq, tq), F32)] * (2 * 2 * hg),
        compiler_params=_params("parallel", "parallel", "arbitrary"),
        name="attn_prompt",
    )(*small, q, k, v, az)


def _attn_sample_body(slope_ref, lq1_ref, lk1_ref, lq2_ref, lk2_ref, subg_ref,
                      q_ref, kp_ref, vp_ref, kn_ref, vn_ref, az_ref, out_ref, m_s, l_s, acc_s,
                      *, past, pc, lambda_init):
    j = pl.program_id(1)
    T = q_ref.shape[0]
    hd = ATTN_V_DIM
    heads = ATTN_HEADS
    cols = [slice(h * hd, (h + 1) * hd) for h in range(heads)]
    slopes = [slope_ref[h, 0:1, 0:1] for h in range(heads)]
    qpos = past + (lax.broadcasted_iota(jnp.int32, (2 * T, 1), 0) & (T - 1))
    qposf = qpos.astype(F32)

    @pl.when(j == 0)
    def _():
        m_s[...] = jnp.full(m_s.shape, NEG_INF, F32)
        l_s[...] = jnp.zeros(l_s.shape, F32)
        acc_s[...] = jnp.zeros(acc_s.shape, F32)

    qs = [jnp.concatenate(_split_q(q_ref[:, cols[h]]), axis=0) for h in range(heads)]

    def attend(keys, vals, bias_of):
        ss = [_scores(qs[h], keys[h]) + bias_of(h) for h in range(heads)]
        for h in range(heads):
            m_s[h], l_s[h], acc_s[h] = _online(ss[h], vals[h], m_s[h], l_s[h], acc_s[h])

    kposf = (j * pc + lax.broadcasted_iota(jnp.int32, (1, pc), 1)).astype(F32)
    attend([kp_ref[pl.ds(h, pc, stride=heads), :].astype(BF16) for h in range(heads)],
           [vp_ref[pl.ds(h, pc, stride=heads), :].astype(BF16) for h in range(heads)],
           lambda h: slopes[h] * kposf - slopes[h] * qposf)

    @pl.when(j == pl.num_programs(1) - 1)
    def _():
        kpos = past + lax.broadcasted_iota(jnp.int32, (1, T), 1)
        dist = jnp.abs(qpos - kpos).astype(F32)
        visible = (kpos >> 6) <= (qpos >> 6)
        attend([kn_ref[:, cols[h]].astype(BF16) for h in range(heads)],
               [vn_ref[:, cols[h]].astype(BF16) for h in range(heads)],
               lambda h: jnp.where(visible, -slopes[h] * dist, NEG_INF))
        for h in range(heads):
            o = acc_s[h] / l_s[h]
            o = _attn_finish(o[0:T], o[T:2 * T], lq1_ref[...], lk1_ref[...], lq2_ref[...], lk2_ref[...],
                             subg_ref[...], az_ref[:, cols[h]], lambda_init)
            out_ref[:, cols[h]] = o.astype(out_ref.dtype)


def _attn_sample(q, k_cache, v_cache, k_stack, v_stack, layer, az, small, lambda_init):
    B, T, W = q.shape
    depth, _, P = k_cache.shape[:3]
    assert P % CHUNK == 0 and T == CHUNK and P > 0
    hd = ATTN_V_DIM
    pc = _row_tile(P, 1024)
    tile = pl.BlockSpec((None, T, W), lambda b, j: (b, 0, 0))
    new = pl.BlockSpec((None, None, T, W), lambda b, j: (layer, b, 0, 0))
    pastb = pl.BlockSpec((None, None, pc * ATTN_HEADS, hd), lambda b, j: (layer, b, j, 0))
    small_specs = [pl.BlockSpec((ATTN_HEADS, 1, 128), lambda b, j: (0, 0, 0))] + _small_specs()[1:]
    rows = lambda a: a.reshape(depth, B, P * ATTN_HEADS, hd)
    return pl.pallas_call(
        functools.partial(_attn_sample_body, past=P, pc=pc, lambda_init=lambda_init),
        grid=(B, P // pc),
        in_specs=small_specs + [tile, pastb, pastb, new, new, tile],
        out_specs=tile,
        out_shape=jax.ShapeDtypeStruct((B, T, W), BF16),
        scratch_shapes=[pltpu.VMEM((ATTN_HEADS, 2 * T, 1), F32), pltpu.VMEM((ATTN_HEADS, 2 * T, 1), F32),
                        pltpu.VMEM((ATTN_HEADS, 2 * T, hd), F32)],
        compiler_params=_params("parallel", "arbitrary"),
        name="attn_sample",
    )(*small, q, rows(k_cache), rows(v_cache), k_stack, v_stack, az)


def _memat_body(q_ref, k_ref, v_ref, out_ref):
    q = q_ref[...]
    k = k_ref[...]
    v = v_ref[...]
    lane = lax.broadcasted_iota(jnp.int32, (1, MEM_WIDTH), 1)
    acc = jnp.zeros(out_ref.shape, F32)
    for h in range(MEM_HEADS):
        mine = (lane >= h * MEM_HEAD_DIM) & (lane < (h + 1) * MEM_HEAD_DIM)
        kh = jnp.where(mine, k, 0.0).astype(BF16)
        vh = jnp.where(mine, v, 0.0).astype(BF16)
        s = _scores(q, kh) * (MEM_HEAD_DIM ** -0.5)
        p = jnp.exp(s - jnp.max(s, axis=-1, keepdims=True))
        a = p / jnp.sum(p, axis=-1, keepdims=True)
        acc = acc + jnp.dot(a.astype(BF16), vh, preferred_element_type=F32)
    out_ref[...] = acc.astype(out_ref.dtype)


def _memat(mq, mk, mv):
    B, T, W = mq.shape
    M = mk.shape[1]
    tm = _row_tile(T, 512)
    return pl.pallas_call(
        _memat_body,
        grid=(B, T // tm),
        in_specs=[pl.BlockSpec((None, tm, W), lambda b, i: (b, i, 0)),
                  pl.BlockSpec((None, M, W), lambda b, i: (b, 0, 0)),
                  pl.BlockSpec((None, M, W), lambda b, i: (b, 0, 0))],
        out_specs=pl.BlockSpec((None, tm, W), lambda b, i: (b, i, 0)),
        out_shape=jax.ShapeDtypeStruct((B, T, W), BF16),
        compiler_params=_params("parallel", "parallel"),
        name="memat",
    )(mq, mk, mv)


def _merge_body(x_ref, g_ref, pool_ref, ssm_ref, attn_ref, mem_ref,
                wp_ref, ws_ref, wa_ref, wm_ref, wg_ref, wo_ref, fg_ref, out_ref, *, final_norm):
    x = x_ref[...]
    D = x.shape[-1]
    h = _rms_rows(x, g_ref[...]).astype(BF16)
    merged = jnp.zeros(x.shape, F32)
    for n, (a_ref, w_ref) in enumerate(((pool_ref, wp_ref), (ssm_ref, ws_ref), (attn_ref, wa_ref), (mem_ref, wm_ref))):
        gate = jax.nn.sigmoid(jnp.dot(h, wg_ref[:, n * D:(n + 1) * D], preferred_element_type=F32))
        merged = merged + gate * jnp.dot(a_ref[...], w_ref[...], preferred_element_type=F32)
    y = x + jnp.dot(merged.astype(BF16), wo_ref[...], preferred_element_type=F32)
    if final_norm:
        y = _rms_rows(y, fg_ref[...])
    out_ref[...] = y


def _merge(x, g, pool_g, ssm_g, attn_g, mem_o, wp, ws, wa, wm, wg, wo, fg, final_norm):
    B, T, D = x.shape
    tm = _row_tile(T, 512)
    bt = lambda wd: pl.BlockSpec((None, tm, wd), lambda b, i: (b, i, 0))
    return pl.pallas_call(
        functools.partial(_merge_body, final_norm=final_norm),
        grid=(B, T // tm),
        in_specs=[bt(D), _const_spec((1, D)), bt(POOL_WIDTH),
                  pl.BlockSpec((tm, SSM_WIDTH), lambda b, i: (i, b)),
                  bt(ATTN_WIDTH), bt(MEM_WIDTH),
                  _const_spec(wp.shape), _const_spec(ws.shape), _const_spec(wa.shape), _const_spec(wm.shape),
                  _const_spec(wg.shape), _const_spec(wo.shape), _const_spec((1, D))],
        out_specs=bt(D),
        out_shape=jax.ShapeDtypeStruct((B, T, D), F32),
        compiler_params=_params("parallel", "parallel"),
        name="merge",
    )(x, g.reshape(1, D), pool_g, ssm_g, attn_g, mem_o, wp, ws, wa, wm, wg, wo, fg.reshape(1, D))


def _trunk_layer(x, layer, depth, pool_hist, ssm_re, ssm_im, caches, kv_stacks, mem_k, mem_v, lw, lambda_init,
                 final_g):
    B, T, D = x.shape
    pu, pz, su, sz, q, k_stack, k16, v_stack, v16, az, mq = _trunk_proj(
        x, lw['norm_g'], lw['w_in_main'], layer, depth, kv_stacks)
    pos0 = 0 if caches is None else caches[0].shape[2]
    hist16 = jnp.concatenate([jnp.zeros((B, 1, POOL_WIDTH), F32), pool_hist], axis=1)
    pool_g, hist_new = _pool(pu, hist16, pz, lw['pool_w_bd'], lw['pool_scale'], pos0)
    ssm_g, sre, sim = _ssm(su.reshape(T * B, SSM_WIDTH), sz.reshape(T * B, SSM_WIDTH), B,
                           lw['ssm_bw'], lw['ssm_cw'], lw['ssm_a_re'], lw['ssm_a_im'], lw['ssm_d'], lw['w_glu'],
                           ssm_re.reshape(B, SSM_STATES), ssm_im.reshape(B, SSM_STATES))
    if caches is None:
        attn_g = _attn_prompt(q, k16, v16, az, lw['attn_small'], lambda_init)
    else:
        attn_g = _attn_sample(q, caches[0], caches[1], k_stack, v_stack, layer, az, lw['attn_small'], lambda_init)
    mem_o = _memat(mq, mem_k, mem_v)
    y = _merge(x, lw['norm_g'], pool_g, ssm_g.reshape(T, B * SSM_WIDTH), attn_g, mem_o,
               lw['w_br_pool'], lw['w_br_ssm'], lw['w_br_attn'], lw['w_br_mem'], lw['w_in_gates'], lw['w_out'],
               final_g, layer == depth - 1)
    return (y, hist_new[:, 1:], sre.reshape(B, SSM_GROUPS, SSM_STATE), sim.reshape(B, SSM_GROUPS, SSM_STATE),
            (k_stack, v_stack))


def kernel(x_prompt, x_sample, mem_prompt, cache_attn_k, cache_attn_v, cache_mem_k, cache_mem_v, state_pool, state_ssm_re, state_ssm_im, norm_g, w_in, pool_w, pool_scale, ssm_lambda_re, ssm_lambda_im, ssm_log_dt, ssm_b_re, ssm_b_im, ssm_c_re, ssm_c_im, ssm_d, ssm_w_glu, attn_lq1, attn_lk1, attn_lq2, attn_lk2, attn_subln_g, mem_norm_g, w_mem_kv, w_br_pool, w_br_ssm, w_br_attn, w_br_mem, w_out, final_norm_g):
    depth = w_in.shape[0]
    B = x_prompt.shape[0]
    Bd = x_sample.shape[0]
    n_main = w_in.shape[-1] - N_BRANCH * x_prompt.shape[-1]
    slopes = jnp.asarray([2.0 ** (-8.0 * (h + 1) / ATTN_HEADS) for h in range(ATTN_HEADS)], F32)
    xp, xs = x_prompt, x_sample
    outs = {n: [] for n in ('mkp', 'mvp', 'poolp', 'srep', 'simp', 'pools', 'sres', 'sims')}
    kv_p = kv_s = None
    for l in range(depth):
        bw, cwm, a_re, a_im = _ssm_weights(ssm_lambda_re[l], ssm_lambda_im[l], ssm_log_dt[l],
                                           ssm_b_re[l], ssm_b_im[l], ssm_c_re[l], ssm_c_im[l])
        eye = jnp.eye(len(POOL_WINDOWS), dtype=F32)
        pool_w_bd = jnp.einsum('gcd,gh->gchd', pool_w[l].astype(F32), eye).reshape(POOL_WIDTH, POOL_WIDTH)
        lw = {
            'norm_g': norm_g[l], 'w_in_main': w_in[l, :, :n_main].astype(BF16),
            'w_in_gates': w_in[l, :, n_main:].astype(BF16),
            'pool_w_bd': pool_w_bd.astype(BF16), 'pool_scale': pool_scale[l],
            'ssm_bw': bw, 'ssm_cw': cwm, 'ssm_a_re': a_re, 'ssm_a_im': a_im, 'ssm_d': ssm_d[l],
            'w_glu': ssm_w_glu[l].astype(BF16),
            'attn_small': _attn_small_inputs(slopes, attn_lq1[l], attn_lk1[l], attn_lq2[l], attn_lk2[l],
                                             attn_subln_g[l]),
            'w_br_pool': w_br_pool[l].astype(BF16), 'w_br_ssm': w_br_ssm[l].astype(BF16),
            'w_br_attn': w_br_attn[l].astype(BF16), 'w_br_mem': w_br_mem[l].astype(BF16),
            'w_out': w_out[l].astype(BF16),
        }
        lambda_init = 0.8 - 0.6 * math.exp(-0.3 * l)
        mk_p, mv_p = _memory_kv(mem_prompt, mem_norm_g[l], w_mem_kv[l].astype(BF16))
        zero_pool = jnp.zeros((B, POOL_HIST, POOL_WIDTH), F32)
        zero_ssm = jnp.zeros((B, SSM_GROUPS, SSM_STATE), F32)
        xp, pool_p, sre_p, sim_p, kv_p = _trunk_layer(
            xp, l, depth, zero_pool, zero_ssm, zero_ssm, None, kv_p, mk_p, mv_p, lw, lambda_init, final_norm_g)
        M = cache_mem_k.shape[2]
        xs, pool_s, sre_s, sim_s, kv_s = _trunk_layer(
            xs, l, depth, state_pool[l], state_ssm_re[l], state_ssm_im[l], (cache_attn_k, cache_attn_v), kv_s,
            cache_mem_k[l].reshape(Bd, M, MEM_WIDTH), cache_mem_v[l].reshape(Bd, M, MEM_WIDTH),
            lw, lambda_init, final_norm_g)
        mshape = (B, mem_prompt.shape[1], MEM_HEADS, MEM_HEAD_DIM)
        for n, a in (('mkp', mk_p.reshape(mshape)), ('mvp', mv_p.reshape(mshape)),
                     ('poolp', pool_p), ('srep', sre_p), ('simp', sim_p),
                     ('pools', pool_s), ('sres', sre_s), ('sims', sim_s)):
            outs[n].append(a)
    st = {n: jnp.stack(a) for n, a in outs.items()}
    heads = lambda a: a.reshape(a.shape[:3] + (ATTN_HEADS, ATTN_V_DIM))
    return (xp, xs, heads(kv_p[0]), heads(kv_p[1]), st['mkp'], st['mvp'], st['poolp'], st['srep'], st['simp'],
            heads(kv_s[0]), heads(kv_s[1]), st['pools'], st['sres'], st['sims'])
```

```python
import functools
import math

import jax
import jax.numpy as jnp
from jax import lax
from jax.experimental import pallas as pl
from jax.experimental.pallas import tpu as pltpu

F32 = jnp.float32
BF16 = jnp.bfloat16

CHUNK = 64
EPS = 1e-6
NEG_INF = -1e30

POOL_WIDTH = 768
POOL_WINDOWS = (2, 4, 8, 16)
POOL_GROUP = POOL_WIDTH // len(POOL_WINDOWS)
POOL_HIST = 15
POOL_PAD = 32

SSM_WIDTH = 768
SSM_GROUP = 16
SSM_GROUPS = SSM_WIDTH // SSM_GROUP
SSM_STATE = 64
SSM_STATES = SSM_GROUPS * SSM_STATE
SSM_LANE_BLOCK = 128
SSM_BLOCKS = SSM_WIDTH // SSM_LANE_BLOCK
SSM_BLOCK_STATES = SSM_STATES // SSM_BLOCKS

ATTN_HEADS = 8
ATTN_HEAD_DIM = 64
ATTN_V_DIM = 128
ATTN_WIDTH = ATTN_HEADS * ATTN_V_DIM

MEM_HEADS = 4
MEM_HEAD_DIM = 64
MEM_WIDTH = MEM_HEADS * MEM_HEAD_DIM

N_BRANCH = 4

VMEM_LIMIT_BYTES = 56 * 1024 * 1024


def _params(*sem):
    return pltpu.CompilerParams(dimension_semantics=sem, vmem_limit_bytes=VMEM_LIMIT_BYTES)


def _const_spec(shape):
    nd = len(shape)
    return pl.BlockSpec(shape, lambda *_: (0,) * nd, pipeline_mode=pl.Buffered(1))


def _rms_rows(x, g):
    return x * lax.rsqrt(jnp.mean(x * x, axis=-1, keepdims=True) + EPS) * g


def _proj_body(x_ref, g_ref, w_ref, *out_refs, segs):
    h = _rms_rows(x_ref[...], g_ref[...]).astype(BF16)
    outs = iter(out_refs)
    for a, b, modes in segs:
        y = jnp.dot(h, w_ref[:, a:b], preferred_element_type=F32)
        for mode in modes:
            o_ref = next(outs)
            if mode == 'slots':
                for s in range(o_ref.shape[0]):
                    o_ref[s] = y.astype(o_ref.dtype)
            elif mode == 'tiles':
                nt, nl, tt, lw = o_ref.shape
                for j in range(nl):
                    o_ref[:, j] = y[:, j * lw:(j + 1) * lw].astype(o_ref.dtype).reshape(nt, tt, lw)
            else:
                o_ref[...] = y.astype(o_ref.dtype)


def _row_tile(t, want):
    tm = min(t, want)
    assert t % tm == 0
    return tm


def _proj(x, g, w, segs, out_shapes, out_specs, tm, carried=None):
    B, T, D = x.shape
    carried = carried or {}

    def body(x_ref, g_ref, w_ref, *refs):
        _proj_body(x_ref, g_ref, w_ref, *refs[len(carried):], segs=segs)

    return pl.pallas_call(
        body,
        grid=(B, T // tm),
        in_specs=[pl.BlockSpec((None, tm, D), lambda b, i: (b, i, 0)),
                  _const_spec((1, D)), _const_spec(w.shape)]
        + [pl.BlockSpec(memory_space=pl.ANY)] * len(carried),
        out_specs=out_specs,
        out_shape=out_shapes,
        input_output_aliases={3 + n: o for n, o in enumerate(carried)},
        compiler_params=_params("parallel", "parallel"),
        name="proj",
    )(x, g.reshape(1, D), w, *carried.values())


def _ssm_time_tile(T, nb):
    return _row_tile(T, max(8, 1024 // nb))


def _trunk_proj(x, g, w, layer, depth, kv_stacks):
    B, T, D = x.shape
    tm = _row_tile(T, 512)
    tt = _ssm_time_tile(T, B)
    bounds = [0, 768, 1536, 2304, 3072, 4096, 5120, 6144, 7168, 7424]
    first = kv_stacks is None
    stack_mode = 'slots' if first else 'rows'
    modes = [('rows',), ('rows',), ('tiles',), ('rows',), ('rows',), (stack_mode, 'rows'), (stack_mode, 'rows'),
             ('rows',), ('rows',)]
    segs = tuple(zip(bounds[:-1], bounds[1:], modes))
    bt = lambda wd, dt: jax.ShapeDtypeStruct((B, T, wd), dt)
    s_bt = lambda wd: pl.BlockSpec((None, tm, wd), lambda b, i: (b, i, 0))
    tiles = jax.ShapeDtypeStruct((T // tt, SSM_BLOCKS, B, tt, SSM_LANE_BLOCK), F32)
    s_tiles = pl.BlockSpec((tm // tt, SSM_BLOCKS, None, tt, SSM_LANE_BLOCK), lambda b, i: (i, 0, b, 0, 0))
    stack = jax.ShapeDtypeStruct((depth, B, T, ATTN_WIDTH), F32)
    if first:
        s_stack = pl.BlockSpec((depth, None, tm, ATTN_WIDTH), lambda b, i: (0, b, i, 0))
    else:
        s_stack = pl.BlockSpec((None, None, tm, ATTN_WIDTH), lambda b, i: (layer, b, i, 0))
    shapes = [bt(768, F32), bt(768, BF16), tiles, bt(768, BF16), bt(1024, BF16),
              stack, bt(1024, BF16), stack, bt(1024, BF16), bt(1024, BF16), bt(256, BF16)]
    specs = [s_bt(768), s_bt(768), s_tiles, s_bt(768), s_bt(1024),
             s_stack, s_bt(1024), s_stack, s_bt(1024), s_bt(1024), s_bt(256)]
    carried = None if first else {5: kv_stacks[0], 7: kv_stacks[1]}
    return _proj(x, g, w, segs, shapes, specs, tm, carried)


def _memory_kv(mem, g, w):
    B, M, D = mem.shape
    segs = ((0, MEM_WIDTH, ('rows',)), (MEM_WIDTH, 2 * MEM_WIDTH, ('rows',)))
    shapes = [jax.ShapeDtypeStruct((B, M, MEM_WIDTH), F32)] * 2
    specs = [pl.BlockSpec((None, M, MEM_WIDTH), lambda b, i: (b, i, 0))] * 2
    return _proj(mem, g, w, segs, shapes, specs, M)


def _pool_body(u_ref, hist_ref, z_ref, w_ref, scale_ref, out_ref, hist_out_ref,
               ext, b2, b4, b8, *, tt, pos0):
    i = pl.program_id(1)
    n = POOL_PAD + tt

    @pl.when(i == 0)
    def _():
        ext[0:16, :] = jnp.zeros((16, POOL_WIDTH), F32)
        ext[16:32, :] = hist_ref[...]

    @pl.when(i > 0)
    def _():
        ext[16:32, :] = ext[tt + 16:tt + 32, :]

    u = u_ref[...]
    ext[32:n, :] = u
    b2[8:n, :] = ext[8:n, :] + ext[7:n - 1, :]
    b4[16:n, :] = b2[16:n, :] + b2[14:n - 2, :]
    b8[24:n, :] = b4[24:n, :] + b4[20:n - 4, :]
    s2 = b2[32:n, :]
    s4 = b4[32:n, :]
    s8 = b8[32:n, :]
    s16 = s8 + b8[24:n - 8, :]
    col = lax.broadcasted_iota(jnp.int32, (1, POOL_WIDTH), 1)
    wsum = jnp.where(col < POOL_GROUP, s2,
                     jnp.where(col < 2 * POOL_GROUP, s4, jnp.where(col < 3 * POOL_GROUP, s8, s16)))
    win = jnp.where(col < POOL_GROUP, 2, jnp.where(col < 2 * POOL_GROUP, 4, jnp.where(col < 3 * POOL_GROUP, 8, 16)))
    pos = pos0 + i * tt + lax.broadcasted_iota(jnp.int32, (tt, 1), 0)
    cnt = jnp.minimum(win, pos + 1).astype(F32)
    diff = (wsum / cnt - u).astype(BF16)
    y = jnp.dot(diff, w_ref[...], preferred_element_type=F32) * scale_ref[...]
    out_ref[...] = (y * jax.nn.silu(z_ref[...].astype(F32))).astype(out_ref.dtype)
    hist_out_ref[...] = ext[tt + 16:tt + 32, :]


def _pool(pu, hist16, pz, w_bd, scale, pos0):
    B, T, C = pu.shape
    tt = _row_tile(T, 512)
    n = POOL_PAD + tt
    return pl.pallas_call(
        functools.partial(_pool_body, tt=tt, pos0=pos0),
        grid=(B, T // tt),
        in_specs=[pl.BlockSpec((None, tt, C), lambda b, i: (b, i, 0)),
                  pl.BlockSpec((None, 16, C), lambda b, i: (b, 0, 0)),
                  pl.BlockSpec((None, tt, C), lambda b, i: (b, i, 0)),
                  _const_spec((C, C)), _const_spec((1, C))],
        out_specs=[pl.BlockSpec((None, tt, C), lambda b, i: (b, i, 0)),
                   pl.BlockSpec((None, 16, C), lambda b, i: (b, 0, 0))],
        out_shape=[jax.ShapeDtypeStruct((B, T, C), BF16), jax.ShapeDtypeStruct((B, 16, C), F32)],
        scratch_shapes=[pltpu.VMEM((n, C), F32)] * 4,
        compiler_params=_params("parallel", "arbitrary"),
        name="pool",
    )(pu, hist16, pz, w_bd, scale.reshape(1, C))


def _ssm_body(u_ref, bw_ref, cw_ref, are_ref, aim_ref, d_ref, wglu_ref, s0re_ref, s0im_ref,
              out_ref, sre_ref, sim_ref, u, bu, y, *, nb, tt, cw):
    i = pl.program_id(0)

    @pl.when(i == 0)
    def _():
        sre_ref[...] = s0re_ref[...]
        sim_ref[...] = s0im_ref[...]

    for j in range(SSM_BLOCKS):
        for t in range(tt):
            u[t * nb:(t + 1) * nb, j * SSM_LANE_BLOCK:(j + 1) * SSM_LANE_BLOCK] = u_ref[j, pl.ds(t, nb, stride=tt), :]

    half = SSM_BLOCK_STATES
    for j in range(SSM_BLOCKS):
        lanes = slice(j * SSM_LANE_BLOCK, (j + 1) * SSM_LANE_BLOCK)
        bu[...] = jnp.dot(u[:, lanes].astype(BF16), bw_ref[j], preferred_element_type=F32)
        for c in range(half // cw):
            st = slice(j * half + c * cw, j * half + (c + 1) * cw)
            cre = slice(c * cw, (c + 1) * cw)
            cim = slice(half + c * cw, half + (c + 1) * cw)
            ar = jnp.broadcast_to(are_ref[:, st], (nb, cw))
            ai = jnp.broadcast_to(aim_ref[:, st], (nb, cw))

            def step(t, carry, cre=cre, cim=cim, ar=ar, ai=ai):
                hr, hi = carry
                rows = pl.ds(pl.multiple_of(t * nb, nb), nb)
                nhr = ar * hr - ai * hi + bu[rows, cre]
                nhi = ar * hi + ai * hr + bu[rows, cim]
                bu[rows, cre] = nhr
                bu[rows, cim] = nhi
                return nhr, nhi

            hr, hi = lax.fori_loop(0, tt, step, (sre_ref[:, st], sim_ref[:, st]))
            sre_ref[:, st] = hr
            sim_ref[:, st] = hi
        y[:, lanes] = jnp.dot(bu[...].astype(BF16), cw_ref[j], preferred_element_type=F32)

    yy = y[...] + d_ref[...] * u[...]
    act = jax.nn.gelu(yy).astype(BF16)
    zz = jnp.dot(act, wglu_ref[...], preferred_element_type=F32)
    y[...] = zz[:, :SSM_WIDTH] * jax.nn.sigmoid(zz[:, SSM_WIDTH:])
    for j in range(SSM_BLOCKS):
        for t in range(tt):
            out_ref[j, pl.ds(t, nb, stride=tt), :] = y[t * nb:(t + 1) * nb, j * SSM_LANE_BLOCK:(j + 1) * SSM_LANE_BLOCK]


def _ssm(su, nb, bw, cw_mat, a_re, a_im, d, w_glu, s0_re, s0_im):
    nt, nl, _, tt, lw = su.shape
    C = nl * lw
    rows = tt * nb
    cw = min(SSM_BLOCK_STATES, 128 * max(1, 64 // nb))
    tile = pl.BlockSpec((None, nl, rows, lw), lambda i: (i, 0, 0, 0))
    y, sre, sim = pl.pallas_call(
        functools.partial(_ssm_body, nb=nb, tt=tt, cw=cw),
        grid=(nt,),
        in_specs=[tile, _const_spec(bw.shape), _const_spec(cw_mat.shape),
                  _const_spec((1, SSM_STATES)), _const_spec((1, SSM_STATES)),
                  _const_spec((1, C)), _const_spec(w_glu.shape),
                  _const_spec((nb, SSM_STATES)), _const_spec((nb, SSM_STATES))],
        out_specs=[tile,
                   pl.BlockSpec((nb, SSM_STATES), lambda i: (0, 0)),
                   pl.BlockSpec((nb, SSM_STATES), lambda i: (0, 0))],
        out_shape=[jax.ShapeDtypeStruct((nt, nl, rows, lw), F32),
                   jax.ShapeDtypeStruct((nb, SSM_STATES), F32),
                   jax.ShapeDtypeStruct((nb, SSM_STATES), F32)],
        scratch_shapes=[pltpu.VMEM((rows, C), F32), pltpu.VMEM((rows, 2 * SSM_BLOCK_STATES), F32),
                        pltpu.VMEM((rows, C), F32)],
        compiler_params=_params("arbitrary"),
        name="ssm",
    )(su.reshape(nt, nl, rows, lw), bw, cw_mat, a_re.reshape(1, -1), a_im.reshape(1, -1), d.reshape(1, C), w_glu,
      s0_re, s0_im)
    return y.reshape(su.shape), sre, sim


def _ssm_weights(lam_re, lam_im, log_dt, b_re, b_im, c_re, c_im):
    dt = jnp.exp(log_dt.astype(F32))[:, None]
    lr, li = lam_re.astype(F32), lam_im.astype(F32)
    mag = jnp.exp(lr * dt)
    ab_re, ab_im = mag * jnp.cos(li * dt), mag * jnp.sin(li * dt)
    nr, ni = ab_re - 1.0, ab_im
    den = lr * lr + li * li
    f_re = (nr * lr + ni * li) / den
    f_im = (ni * lr - nr * li) / den
    br, bi = b_re.astype(F32), b_im.astype(F32)
    bb_re = f_re[..., None] * br - f_im[..., None] * bi
    bb_im = f_re[..., None] * bi + f_im[..., None] * br
    gpb = SSM_GROUPS // SSM_BLOCKS
    eye = jnp.eye(gpb, dtype=F32)

    def in_block(bb):
        bb = bb.reshape(SSM_BLOCKS, gpb, SSM_STATE, SSM_GROUP)
        m = jnp.einsum('jgpc,gh->jgchp', bb, eye)
        return m.reshape(SSM_BLOCKS, gpb * SSM_GROUP, gpb * SSM_STATE)

    def out_block(cc):
        cc = cc.reshape(SSM_BLOCKS, gpb, SSM_GROUP, SSM_STATE)
        m = jnp.einsum('jgcp,gh->jgphc', cc, eye)
        return m.reshape(SSM_BLOCKS, gpb * SSM_STATE, gpb * SSM_GROUP)

    bw = jnp.concatenate([in_block(bb_re), in_block(bb_im)], axis=-1).astype(BF16)
    cwm = jnp.concatenate([out_block(c_re.astype(F32)), -out_block(c_im.astype(F32))], axis=1).astype(BF16)
    return bw, cwm, ab_re.reshape(-1), ab_im.reshape(-1)


def _split_q(q):
    qf = q.astype(F32) * (ATTN_HEAD_DIM ** -0.5)
    lane = lax.broadcasted_iota(jnp.int32, (1, ATTN_V_DIM), 1)
    first = lane < ATTN_HEAD_DIM
    return jnp.where(first, qf, 0.0).astype(BF16), jnp.where(first, 0.0, qf).astype(BF16)


def _scores(q, k):
    return lax.dot_general(q, k, (((1,), (1,)), ((), ())), preferred_element_type=F32)


def _online(t, v, m, l, acc):
    m_new = jnp.maximum(m, jnp.max(t, axis=-1, keepdims=True))
    alpha = jnp.exp(m - m_new)
    p = jnp.exp(t - m_new)
    l_new = alpha * l + jnp.sum(p, axis=-1, keepdims=True)
    acc_new = alpha * acc + jnp.dot(p.astype(BF16), v, preferred_element_type=F32)
    return m_new, l_new, acc_new


def _attn_finish(o1, o2, lq1, lk1, lq2, lk2, subg, az, lambda_init):
    lam = (jnp.exp(jnp.sum(lq1 * lk1, axis=-1, keepdims=True))
           - jnp.exp(jnp.sum(lq2 * lk2, axis=-1, keepdims=True)) + lambda_init)
    o = o1 - lam * o2
    o = _rms_rows(o, subg) * (1.0 - lambda_init)
    return o * jax.nn.silu(az.astype(F32))


ATTN_ONES_ROWS = 16
ATTN_HEAD_GROUP = 4


def _online_t(t, vx, m, acc, shift):
    m_new = jnp.maximum(m, jnp.max(t, axis=0, keepdims=True) + shift)
    alpha = jnp.exp(m - m_new)
    p = jnp.exp(t + (shift - m_new))
    return m_new, alpha * acc + jnp.dot(vx, p.astype(BF16), preferred_element_type=F32)


def _attn_prompt_body(slope_ref, lq1_ref, lk1_ref, lq2_ref, lk2_ref, subg_ref,
                      q_ref, k_ref, v_ref, az_ref, out_ref, kx, vxt, corr, wq, *state,
                      tq, nk, heads, lambda_init):
    qi = pl.program_id(2)
    hd = ATTN_V_DIM
    rows_acc = hd + ATTN_ONES_ROWS
    lane = lax.broadcasted_iota(jnp.int32, (tq, hd), 1)
    row = lax.broadcasted_iota(jnp.int32, (tq, hd), 0).astype(F32)
    slopes = [slope_ref[g, 0:1, 0:1] for g in range(heads)]
    cols = [slice(g * hd, (g + 1) * hd) for g in range(heads)]
    nc = 2 * heads
    mst, tmax_a, tmax_b, ast, t_a, t_b = (state[n * nc:(n + 1) * nc] for n in range(6))
    tmax_s, t_s = (tmax_a, tmax_b), (t_a, t_b)

    @pl.when(qi == 0)
    def _():
        ones = jnp.ones((ATTN_ONES_ROWS, tq), BF16)
        c = lax.broadcasted_iota(jnp.int32, (tq, tq), 0)
        r = lax.broadcasted_iota(jnp.int32, (tq, tq), 1)
        visible = (c >> 6) <= (r >> 6)
        ahead = jnp.where(c > r, (c - r).astype(F32), 0.0)
        for g in range(heads):
            kfeat = jnp.where(lane == 0, row, jnp.where(lane == 1, -slopes[g], 0.0)).astype(BF16)
            for j in range(nk):
                rows = slice(j * tq, (j + 1) * tq)
                kx[g, rows, 0:hd] = k_ref[rows, cols[g]]
                kx[g, rows, hd:2 * hd] = kfeat
                vxt[g, j, 0:hd, :] = v_ref[rows, cols[g]].astype(F32).T.astype(BF16)
                vxt[g, j, hd:rows_acc, :] = ones
            corr[g] = jnp.where(visible, -2.0 * slopes[g] * ahead, NEG_INF)

    first = lane < ATTN_HEAD_DIM
    for g in range(heads):
        qf = q_ref[:, cols[g]].astype(F32) * (ATTN_HEAD_DIM ** -0.5)
        qfeat = jnp.where(lane == 0, slopes[g], jnp.where(lane == 1, row, 0.0))
        wq[2 * g] = jnp.concatenate([jnp.where(first, qf, 0.0), qfeat], axis=1).T.astype(BF16)
        wq[2 * g + 1] = jnp.concatenate([jnp.where(first, 0.0, qf), qfeat], axis=1).T.astype(BF16)
    chains = [(g, 2 * g + n) for g in range(heads) for n in range(2)]
    for _, c in chains:
        mst[c][...] = jnp.full((1, tq), NEG_INF, F32)
        ast[c][...] = jnp.zeros((rows_acc, tq), F32)

    def scores(j, buf):
        kstart = pl.multiple_of(j * tq, tq)
        for g, c in chains:
            t = jnp.dot(kx[g, pl.ds(kstart, tq), :], wq[c], preferred_element_type=F32)
            t_s[buf][c][...] = t
            tmax_s[buf][c][...] = jnp.max(t, axis=0, keepdims=True)

    def update(j, buf, diagonal=False):
        for g, c in chains:
            if diagonal:
                t = t_s[buf][c][...] + corr[g]
                tmax = jnp.max(t, axis=0, keepdims=True)
                shift = jnp.zeros((1, 1), F32)
            else:
                t, tmax = t_s[buf][c][...], tmax_s[buf][c][...]
                shift = -slopes[g] * ((qi - j) * tq).astype(F32)
            m_old = mst[c][...]
            m_new = jnp.maximum(m_old, tmax + shift)
            mst[c][...] = m_new
            p = jnp.exp(t + (shift - m_new)).astype(BF16)
            ast[c][...] = jnp.exp(m_old - m_new) * ast[c][...] + jnp.dot(vxt[g, j], p,
                                                                         preferred_element_type=F32)

    def step(j, buf):
        scores(j + 1, 1 - buf)
        update(j, buf)

    def pair(k, carry):
        step(2 * k, 0)
        step(2 * k + 1, 1)
        return carry

    scores(0, 0)
    lax.fori_loop(0, qi // 2, pair, 0)

    @pl.when(qi % 2 == 1)
    def _():
        step(qi - 1, 0)
        update(qi, 1, diagonal=True)

    @pl.when(qi % 2 == 0)
    def _():
        update(qi, 0, diagonal=True)

    lam = (jnp.exp(jnp.sum(lq1_ref[...] * lk1_ref[...], axis=-1, keepdims=True))
           - jnp.exp(jnp.sum(lq2_ref[...] * lk2_ref[...], axis=-1, keepdims=True)) + lambda_init)
    for g in range(heads):
        a1, a2 = ast[2 * g][...], ast[2 * g + 1][...]
        o = (a1[0:hd, :] / a1[hd:hd + 1, :] - lam * (a2[0:hd, :] / a2[hd:hd + 1, :])).T
        o = _rms_rows(o, subg_ref[...]) * (1.0 - lambda_init)
        out_ref[:, cols[g]] = (o * jax.nn.silu(az_ref[:, cols[g]].astype(F32))).astype(out_ref.dtype)


def _attn_small_inputs(slopes, lq1, lk1, lq2, lk2, subg):
    slope_rows = jnp.broadcast_to(slopes[:, None, None], (ATTN_HEADS, 1, 128))
    r = lambda a: a.astype(F32).reshape(1, -1)
    return slope_rows, r(lq1), r(lk1), r(lq2), r(lk2), r(subg)


def _small_specs():
    return [pl.BlockSpec((None, 1, 128), lambda b, h, *_: (h, 0, 0)),
            _const_spec((1, ATTN_HEAD_DIM)), _const_spec((1, ATTN_HEAD_DIM)),
            _const_spec((1, ATTN_HEAD_DIM)), _const_spec((1, ATTN_HEAD_DIM)),
            _const_spec((1, ATTN_V_DIM))]


def _attn_prompt(q, k, v, az, small, lambda_init):
    B, T, W = q.shape
    tq = _row_tile(T, 256)
    assert tq % CHUNK == 0 and tq <= 256
    nk = T // tq
    hd = ATTN_V_DIM
    hg = ATTN_HEAD_GROUP
    rows_acc = hd + ATTN_ONES_ROWS
    tile = pl.BlockSpec((None, tq, hg * hd), lambda b, h, i: (b, i, h))
    full = pl.BlockSpec((None, T, hg * hd), lambda b, h, i: (b, 0, h))
    small_specs = [pl.BlockSpec((hg, 1, 128), lambda b, h, i: (h, 0, 0))] + _small_specs()[1:]
    return pl.pallas_call(
        functools.partial(_attn_prompt_body, tq=tq, nk=nk, heads=hg, lambda_init=lambda_init),
        grid=(B, ATTN_HEADS // hg, nk),
        in_specs=small_specs + [tile, full, full, tile],
        out_specs=tile,
        out_shape=jax.ShapeDtypeStruct((B, T, W), BF16),
        scratch_shapes=[pltpu.VMEM((hg, T, 2 * hd), BF16),
                        pltpu.VMEM((hg, nk, rows_acc, tq), BF16),
                        pltpu.VMEM((hg, tq, tq), F32),
                        pltpu.VMEM((2 * hg, 2 * hd, tq), BF16)]
        + [pltpu.VMEM((1, tq), F32)] * (3 * 2 * hg) + [pltpu.VMEM((rows_acc, tq), F32)] * (2 * hg)
        + [pltpu.VMEM((tq, tq), F32)] * (2 * 2 * hg),
        compiler_params=_params("parallel", "parallel", "arbitrary"),
        name="attn_prompt",
    )(*small, q, k, v, az)


def _attn_sample_body(slope_ref, lq1_ref, lk1_ref, lq2_ref, lk2_ref, subg_ref,
                      q_ref, kp_ref, vp_ref, kn_ref, vn_ref, az_ref, out_ref, m_s, l_s, acc_s,
                      *, past, pc, lambda_init):
    j = pl.program_id(1)
    T = q_ref.shape[0]
    hd = ATTN_V_DIM
    heads = ATTN_HEADS
    cols = [slice(h * hd, (h + 1) * hd) for h in range(heads)]
    slopes = [slope_ref[h, 0:1, 0:1] for h in range(heads)]
    qpos = past + (lax.broadcasted_iota(jnp.int32, (2 * T, 1), 0) & (T - 1))
    qposf = qpos.astype(F32)

    @pl.when(j == 0)
    def _():
        m_s[...] = jnp.full(m_s.shape, NEG_INF, F32)
        l_s[...] = jnp.zeros(l_s.shape, F32)
        acc_s[...] = jnp.zeros(acc_s.shape, F32)

    qs = [jnp.concatenate(_split_q(q_ref[:, cols[h]]), axis=0) for h in range(heads)]

    def attend(keys, vals, bias_of):
        ss = [_scores(qs[h], keys[h]) + bias_of(h) for h in range(heads)]
        for h in range(heads):
            m_s[h], l_s[h], acc_s[h] = _online(ss[h], vals[h], m_s[h], l_s[h], acc_s[h])

    kposf = (j * pc + lax.broadcasted_iota(jnp.int32, (1, pc), 1)).astype(F32)
    attend([kp_ref[pl.ds(h, pc, stride=heads), :].astype(BF16) for h in range(heads)],
           [vp_ref[pl.ds(h, pc, stride=heads), :].astype(BF16) for h in range(heads)],
           lambda h: slopes[h] * kposf - slopes[h] * qposf)

    @pl.when(j == pl.num_programs(1) - 1)
    def _():
        kpos = past + lax.broadcasted_iota(jnp.int32, (1, T), 1)
        dist = jnp.abs(qpos - kpos).astype(F32)
        visible = (kpos >> 6) <= (qpos >> 6)
        attend([kn_ref[:, cols[h]].astype(BF16) for h in range(heads)],
               [vn_ref[:, cols[h]].astype(BF16) for h in range(heads)],
               lambda h: jnp.where(visible, -slopes[h] * dist, NEG_INF))
        for h in range(heads):
            o = acc_s[h] / l_s[h]
            o = _attn_finish(o[0:T], o[T:2 * T], lq1_ref[...], lk1_ref[...], lq2_ref[...], lk2_ref[...],
                             subg_ref[...], az_ref[:, cols[h]], lambda_init)
            out_ref[:, cols[h]] = o.astype(out_ref.dtype)


def _attn_sample(q, k_cache, v_cache, k_stack, v_stack, layer, az, small, lambda_init):
    B, T, W = q.shape
    depth, _, P = k_cache.shape[:3]
    assert P % CHUNK == 0 and T == CHUNK and P > 0
    hd = ATTN_V_DIM
    pc = _row_tile(P, 1024)
    tile = pl.BlockSpec((None, T, W), lambda b, j: (b, 0, 0))
    new = pl.BlockSpec((None, None, T, W), lambda b, j: (layer, b, 0, 0))
    pastb = pl.BlockSpec((None, None, pc * ATTN_HEADS, hd), lambda b, j: (layer, b, j, 0))
    small_specs = [pl.BlockSpec((ATTN_HEADS, 1, 128), lambda b, j: (0, 0, 0))] + _small_specs()[1:]
    rows = lambda a: a.reshape(depth, B, P * ATTN_HEADS, hd)
    return pl.pallas_call(
        functools.partial(_attn_sample_body, past=P, pc=pc, lambda_init=lambda_init),
        grid=(B, P // pc),
        in_specs=small_specs + [tile, pastb, pastb, new, new, tile],
        out_specs=tile,
        out_shape=jax.ShapeDtypeStruct((B, T, W), BF16),
        scratch_shapes=[pltpu.VMEM((ATTN_HEADS, 2 * T, 1), F32), pltpu.VMEM((ATTN_HEADS, 2 * T, 1), F32),
                        pltpu.VMEM((ATTN_HEADS, 2 * T, hd), F32)],
        compiler_params=_params("parallel", "arbitrary"),
        name="attn_sample",
    )(*small, q, rows(k_cache), rows(v_cache), k_stack, v_stack, az)


def _memat_body(q_ref, k_ref, v_ref, out_ref):
    q = q_ref[...]
    k = k_ref[...]
    v = v_ref[...]
    lane = lax.broadcasted_iota(jnp.int32, (1, MEM_WIDTH), 1)
    acc = jnp.zeros(out_ref.shape, F32)
    for h in range(MEM_HEADS):
        mine = (lane >= h * MEM_HEAD_DIM) & (lane < (h + 1) * MEM_HEAD_DIM)
        kh = jnp.where(mine, k, 0.0).astype(BF16)
        vh = jnp.where(mine, v, 0.0).astype(BF16)
        s = _scores(q, kh) * (MEM_HEAD_DIM ** -0.5)
        p = jnp.exp(s - jnp.max(s, axis=-1, keepdims=True))
        a = p / jnp.sum(p, axis=-1, keepdims=True)
        acc = acc + jnp.dot(a.astype(BF16), vh, preferred_element_type=F32)
    out_ref[...] = acc.astype(out_ref.dtype)


def _memat(mq, mk, mv):
    B, T, W = mq.shape
    M = mk.shape[1]
    tm = _row_tile(T, 512)
    return pl.pallas_call(
        _memat_body,
        grid=(B, T // tm),
        in_specs=[pl.BlockSpec((None, tm, W), lambda b, i: (b, i, 0)),
                  pl.BlockSpec((None, M, W), lambda b, i: (b, 0, 0)),
                  pl.BlockSpec((None, M, W), lambda b, i: (b, 0, 0))],
        out_specs=pl.BlockSpec((None, tm, W), lambda b, i: (b, i, 0)),
        out_shape=jax.ShapeDtypeStruct((B, T, W), BF16),
        compiler_params=_params("parallel", "parallel"),
        name="memat",
    )(mq, mk, mv)


def _merge_body(x_ref, g_ref, pool_ref, ssm_ref, sz_ref, attn_ref, mem_ref,
                wp_ref, ws_ref, wa_ref, wm_ref, wg_ref, wo_ref, fg_ref, out_ref, *, final_norm):
    x = x_ref[...]
    D = x.shape[-1]
    h = _rms_rows(x, g_ref[...]).astype(BF16)
    rows = sz_ref.shape[0]
    ssm_y = jnp.concatenate([ssm_ref[:, j].reshape(rows, -1) for j in range(ssm_ref.shape[1])], axis=1)
    ssm_g = (ssm_y * jax.nn.silu(sz_ref[...].astype(F32))).astype(BF16)
    merged = jnp.zeros(x.shape, F32)
    branches = ((pool_ref[...], wp_ref), (ssm_g, ws_ref), (attn_ref[...], wa_ref), (mem_ref[...], wm_ref))
    for n, (a, w_ref) in enumerate(branches):
        gate = jax.nn.sigmoid(jnp.dot(h, wg_ref[:, n * D:(n + 1) * D], preferred_element_type=F32))
        merged = merged + gate * jnp.dot(a, w_ref[...], preferred_element_type=F32)
    y = x + jnp.dot(merged.astype(BF16), wo_ref[...], preferred_element_type=F32)
    if final_norm:
        y = _rms_rows(y, fg_ref[...])
    out_ref[...] = y


def _merge(x, g, pool_g, ssm_y, sz, attn_g, mem_o, wp, ws, wa, wm, wg, wo, fg, final_norm):
    B, T, D = x.shape
    tm = _row_tile(T, 512)
    tt = ssm_y.shape[3]
    bt = lambda wd: pl.BlockSpec((None, tm, wd), lambda b, i: (b, i, 0))
    return pl.pallas_call(
        functools.partial(_merge_body, final_norm=final_norm),
        grid=(B, T // tm),
        in_specs=[bt(D), _const_spec((1, D)), bt(POOL_WIDTH),
                  pl.BlockSpec((tm // tt, SSM_BLOCKS, None, tt, SSM_LANE_BLOCK), lambda b, i: (i, 0, b, 0, 0)),
                  bt(SSM_WIDTH),
                  bt(ATTN_WIDTH), bt(MEM_WIDTH),
                  _const_spec(wp.shape), _const_spec(ws.shape), _const_spec(wa.shape), _const_spec(wm.shape),
                  _const_spec(wg.shape), _const_spec(wo.shape), _const_spec((1, D))],
        out_specs=bt(D),
        out_shape=jax.ShapeDtypeStruct((B, T, D), F32),
        compiler_params=_params("parallel", "parallel"),
        name="merge",
    )(x, g.reshape(1, D), pool_g, ssm_y, sz, attn_g, mem_o, wp, ws, wa, wm, wg, wo, fg.reshape(1, D))


def _trunk_layer(x, layer, depth, pool_hist, ssm_re, ssm_im, caches, kv_stacks, mem_k, mem_v, lw, lambda_init,
                 final_g):
    B, T, D = x.shape
    pu, pz, su, sz, q, k_stack, k16, v_stack, v16, az, mq = _trunk_proj(
        x, lw['norm_g'], lw['w_in_main'], layer, depth, kv_stacks)
    pos0 = 0 if caches is None else caches[0].shape[2]
    hist16 = jnp.concatenate([jnp.zeros((B, 1, POOL_WIDTH), F32), pool_hist], axis=1)
    pool_g, hist_new = _pool(pu, hist16, pz, lw['pool_w_bd'], lw['pool_scale'], pos0)
    ssm_y, sre, sim = _ssm(su, B, lw['ssm_bw'], lw['ssm_cw'], lw['ssm_a_re'], lw['ssm_a_im'], lw['ssm_d'],
                           lw['w_glu'], ssm_re.reshape(B, SSM_STATES), ssm_im.reshape(B, SSM_STATES))
    if caches is None:
        attn_g = _attn_prompt(q, k16, v16, az, lw['attn_small'], lambda_init)
    else:
        attn_g = _attn_sample(q, caches[0], caches[1], k_stack, v_stack, layer, az, lw['attn_small'], lambda_init)
    mem_o = _memat(mq, mem_k, mem_v)
    y = _merge(x, lw['norm_g'], pool_g, ssm_y, sz, attn_g, mem_o,
               lw['w_br_pool'], lw['w_br_ssm'], lw['w_br_attn'], lw['w_br_mem'], lw['w_in_gates'], lw['w_out'],
               final_g, layer == depth - 1)
    return (y, hist_new[:, 1:], sre.reshape(B, SSM_GROUPS, SSM_STATE), sim.reshape(B, SSM_GROUPS, SSM_STATE),
            (k_stack, v_stack))


def kernel(x_prompt, x_sample, mem_prompt, cache_attn_k, cache_attn_v, cache_mem_k, cache_mem_v, state_pool, state_ssm_re, state_ssm_im, norm_g, w_in, pool_w, pool_scale, ssm_lambda_re, ssm_lambda_im, ssm_log_dt, ssm_b_re, ssm_b_im, ssm_c_re, ssm_c_im, ssm_d, ssm_w_glu, attn_lq1, attn_lk1, attn_lq2, attn_lk2, attn_subln_g, mem_norm_g, w_mem_kv, w_br_pool, w_br_ssm, w_br_attn, w_br_mem, w_out, final_norm_g):
    depth = w_in.shape[0]
    B = x_prompt.shape[0]
    Bd = x_sample.shape[0]
    n_main = w_in.shape[-1] - N_BRANCH * x_prompt.shape[-1]
    slopes = jnp.asarray([2.0 ** (-8.0 * (h + 1) / ATTN_HEADS) for h in range(ATTN_HEADS)], F32)
    xp, xs = x_prompt, x_sample
    outs = {n: [] for n in ('mkp', 'mvp', 'poolp', 'srep', 'simp', 'pools', 'sres', 'sims')}
    kv_p = kv_s = None
    for l in range(depth):
        bw, cwm, a_re, a_im = _ssm_weights(ssm_lambda_re[l], ssm_lambda_im[l], ssm_log_dt[l],
                                           ssm_b_re[l], ssm_b_im[l], ssm_c_re[l], ssm_c_im[l])
        eye = jnp.eye(len(POOL_WINDOWS), dtype=F32)
        pool_w_bd = jnp.einsum('gcd,gh->gchd', pool_w[l].astype(F32), eye).reshape(POOL_WIDTH, POOL_WIDTH)
        lw = {
            'norm_g': norm_g[l], 'w_in_main': w_in[l, :, :n_main].astype(BF16),
            'w_in_gates': w_in[l, :, n_main:].astype(BF16),
            'pool_w_bd': pool_w_bd.astype(BF16), 'pool_scale': pool_scale[l],
            'ssm_bw': bw, 'ssm_cw': cwm, 'ssm_a_re': a_re, 'ssm_a_im': a_im, 'ssm_d': ssm_d[l],
            'w_glu': ssm_w_glu[l].astype(BF16),
            'attn_small': _attn_small_inputs(slopes, attn_lq1[l], attn_lk1[l], attn_lq2[l], attn_lk2[l],
                                             attn_subln_g[l]),
            'w_br_pool': w_br_pool[l].astype(BF16), 'w_br_ssm': w_br_ssm[l].astype(BF16),
            'w_br_attn': w_br_attn[l].astype(BF16), 'w_br_mem': w_br_mem[l].astype(BF16),
            'w_out': w_out[l].astype(BF16),
        }
        lambda_init = 0.8 - 0.6 * math.exp(-0.3 * l)
        mk_p, mv_p = _memory_kv(mem_prompt, mem_norm_g[l], w_mem_kv[l].astype(BF16))
        zero_pool = jnp.zeros((B, POOL_HIST, POOL_WIDTH), F32)
        zero_ssm = jnp.zeros((B, SSM_GROUPS, SSM_STATE), F32)
        xp, pool_p, sre_p, sim_p, kv_p = _trunk_layer(
            xp, l, depth, zero_pool, zero_ssm, zero_ssm, None, kv_p, mk_p, mv_p, lw, lambda_init, final_norm_g)
        M = cache_mem_k.shape[2]
        xs, pool_s, sre_s, sim_s, kv_s = _trunk_layer(
            xs, l, depth, state_pool[l], state_ssm_re[l], state_ssm_im[l], (cache_attn_k, cache_attn_v), kv_s,
            cache_mem_k[l].reshape(Bd, M, MEM_WIDTH), cache_mem_v[l].reshape(Bd, M, MEM_WIDTH),
            lw, lambda_init, final_norm_g)
        mshape = (B, mem_prompt.shape[1], MEM_HEADS, MEM_HEAD_DIM)
        for n, a in (('mkp', mk_p.reshape(mshape)), ('mvp', mv_p.reshape(mshape)),
                     ('poolp', pool_p), ('srep', sre_p), ('simp', sim_p),
                     ('pools', pool_s), ('sres', sre_s), ('sims', sim_s)):
            outs[n].append(a)
    st = {n: jnp.stack(a) for n, a in outs.items()}
    heads = lambda a: a.reshape(a.shape[:3] + (ATTN_HEADS, ATTN_V_DIM))
    return (xp, xs, heads(kv_p[0]), heads(kv_p[1]), st['mkp'], st['mvp'], st['poolp'], st['srep'], st['simp'],
            heads(kv_s[0]), heads(kv_s[1]), st['pools'], st['sres'], st['sims'])
```

```python
import functools
import math

import jax
import jax.numpy as jnp
from jax import lax
from jax.experimental import pallas as pl
from jax.experimental.pallas import tpu as pltpu

F32 = jnp.float32
BF16 = jnp.bfloat16

CHUNK = 64
EPS = 1e-6
NEG_INF = -1e30

POOL_WIDTH = 768
POOL_WINDOWS = (2, 4, 8, 16)
POOL_GROUP = POOL_WIDTH // len(POOL_WINDOWS)
POOL_HIST = 15
POOL_PAD = 32

SSM_WIDTH = 768
SSM_GROUP = 16
SSM_GROUPS = SSM_WIDTH // SSM_GROUP
SSM_STATE = 64
SSM_STATES = SSM_GROUPS * SSM_STATE
SSM_LANE_BLOCK = 128
SSM_BLOCKS = SSM_WIDTH // SSM_LANE_BLOCK
SSM_BLOCK_STATES = SSM_STATES // SSM_BLOCKS

ATTN_HEADS = 8
ATTN_HEAD_DIM = 64
ATTN_V_DIM = 128
ATTN_WIDTH = ATTN_HEADS * ATTN_V_DIM

MEM_HEADS = 4
MEM_HEAD_DIM = 64
MEM_WIDTH = MEM_HEADS * MEM_HEAD_DIM

N_BRANCH = 4

VMEM_LIMIT_BYTES = 56 * 1024 * 1024


def _params(*sem):
    return pltpu.CompilerParams(dimension_semantics=sem, vmem_limit_bytes=VMEM_LIMIT_BYTES)


def _const_spec(shape):
    nd = len(shape)
    return pl.BlockSpec(shape, lambda *_: (0,) * nd, pipeline_mode=pl.Buffered(1))


def _rms_rows(x, g):
    return x * lax.rsqrt(jnp.mean(x * x, axis=-1, keepdims=True) + EPS) * g


def _proj_body(x_ref, g_ref, w_ref, *out_refs, segs):
    h = _rms_rows(x_ref[...], g_ref[...]).astype(BF16)
    outs = iter(out_refs)
    for a, b, modes in segs:
        y = jnp.dot(h, w_ref[:, a:b], preferred_element_type=F32)
        for mode in modes:
            o_ref = next(outs)
            if mode == 'slots':
                for s in range(o_ref.shape[0]):
                    o_ref[s] = y.astype(o_ref.dtype)
            elif mode == 'tiles':
                nt, nl, tt, lw = o_ref.shape
                for j in range(nl):
                    o_ref[:, j] = y[:, j * lw:(j + 1) * lw].astype(o_ref.dtype).reshape(nt, tt, lw)
            else:
                o_ref[...] = y.astype(o_ref.dtype)


def _row_tile(t, want):
    tm = min(t, want)
    assert t % tm == 0
    return tm


def _proj(x, g, w, segs, out_shapes, out_specs, tm, carried=None):
    B, T, D = x.shape
    carried = carried or {}

    def body(x_ref, g_ref, w_ref, *refs):
        _proj_body(x_ref, g_ref, w_ref, *refs[len(carried):], segs=segs)

    return pl.pallas_call(
        body,
        grid=(B, T // tm),
        in_specs=[pl.BlockSpec((None, tm, D), lambda b, i: (b, i, 0)),
                  _const_spec((1, D)), _const_spec(w.shape)]
        + [pl.BlockSpec(memory_space=pl.ANY)] * len(carried),
        out_specs=out_specs,
        out_shape=out_shapes,
        input_output_aliases={3 + n: o for n, o in enumerate(carried)},
        compiler_params=_params("parallel", "parallel"),
        name="proj",
    )(x, g.reshape(1, D), w, *carried.values())


def _ssm_time_tile(T, nb):
    return _row_tile(T, max(8, 1024 // nb))


def _trunk_proj(x, g, w, layer, depth, kv_stacks):
    B, T, D = x.shape
    tm = _row_tile(T, 512)
    tt = _ssm_time_tile(T, B)
    bounds = [0, 768, 1536, 2304, 3072, 4096, 5120, 6144, 7168, 7424]
    first = kv_stacks is None
    stack_mode = 'slots' if first else 'rows'
    modes = [('rows',), ('rows',), ('tiles',), ('rows',), ('rows',), (stack_mode, 'rows'), (stack_mode, 'rows'),
             ('rows',), ('rows',)]
    segs = tuple(zip(bounds[:-1], bounds[1:], modes))
    bt = lambda wd, dt: jax.ShapeDtypeStruct((B, T, wd), dt)
    s_bt = lambda wd: pl.BlockSpec((None, tm, wd), lambda b, i: (b, i, 0))
    tiles = jax.ShapeDtypeStruct((T // tt, SSM_BLOCKS, B, tt, SSM_LANE_BLOCK), F32)
    s_tiles = pl.BlockSpec((tm // tt, SSM_BLOCKS, None, tt, SSM_LANE_BLOCK), lambda b, i: (i, 0, b, 0, 0))
    stack = jax.ShapeDtypeStruct((depth, B, T, ATTN_WIDTH), F32)
    if first:
        s_stack = pl.BlockSpec((depth, None, tm, ATTN_WIDTH), lambda b, i: (0, b, i, 0))
    else:
        s_stack = pl.BlockSpec((None, None, tm, ATTN_WIDTH), lambda b, i: (layer, b, i, 0))
    shapes = [bt(768, F32), bt(768, BF16), tiles, bt(768, BF16), bt(1024, BF16),
              stack, bt(1024, BF16), stack, bt(1024, BF16), bt(1024, BF16), bt(256, BF16)]
    specs = [s_bt(768), s_bt(768), s_tiles, s_bt(768), s_bt(1024),
             s_stack, s_bt(1024), s_stack, s_bt(1024), s_bt(1024), s_bt(256)]
    carried = None if first else {5: kv_stacks[0], 7: kv_stacks[1]}
    return _proj(x, g, w, segs, shapes, specs, tm, carried)


def _memory_kv(mem, g, w):
    B, M, D = mem.shape
    segs = ((0, MEM_WIDTH, ('rows',)), (MEM_WIDTH, 2 * MEM_WIDTH, ('rows',)))
    shapes = [jax.ShapeDtypeStruct((B, M, MEM_WIDTH), F32)] * 2
    specs = [pl.BlockSpec((None, M, MEM_WIDTH), lambda b, i: (b, i, 0))] * 2
    return _proj(mem, g, w, segs, shapes, specs, M)


def _pool_body(u_ref, hist_ref, z_ref, w_ref, scale_ref, out_ref, hist_out_ref,
               ext, b2, b4, b8, *, tt, pos0):
    i = pl.program_id(1)
    n = POOL_PAD + tt

    @pl.when(i == 0)
    def _():
        ext[0:16, :] = jnp.zeros((16, POOL_WIDTH), F32)
        ext[16:32, :] = hist_ref[...]

    @pl.when(i > 0)
    def _():
        ext[16:32, :] = ext[tt + 16:tt + 32, :]

    u = u_ref[...]
    ext[32:n, :] = u
    b2[8:n, :] = ext[8:n, :] + ext[7:n - 1, :]
    b4[16:n, :] = b2[16:n, :] + b2[14:n - 2, :]
    b8[24:n, :] = b4[24:n, :] + b4[20:n - 4, :]
    s2 = b2[32:n, :]
    s4 = b4[32:n, :]
    s8 = b8[32:n, :]
    s16 = s8 + b8[24:n - 8, :]
    col = lax.broadcasted_iota(jnp.int32, (1, POOL_WIDTH), 1)
    wsum = jnp.where(col < POOL_GROUP, s2,
                     jnp.where(col < 2 * POOL_GROUP, s4, jnp.where(col < 3 * POOL_GROUP, s8, s16)))
    win = jnp.where(col < POOL_GROUP, 2, jnp.where(col < 2 * POOL_GROUP, 4, jnp.where(col < 3 * POOL_GROUP, 8, 16)))
    pos = pos0 + i * tt + lax.broadcasted_iota(jnp.int32, (tt, 1), 0)
    cnt = jnp.minimum(win, pos + 1).astype(F32)
    diff = (wsum / cnt - u).astype(BF16)
    y = jnp.dot(diff, w_ref[...], preferred_element_type=F32) * scale_ref[...]
    out_ref[...] = (y * jax.nn.silu(z_ref[...].astype(F32))).astype(out_ref.dtype)
    hist_out_ref[...] = ext[tt + 16:tt + 32, :]


def _pool(pu, hist16, pz, w_bd, scale, pos0):
    B, T, C = pu.shape
    tt = _row_tile(T, 512)
    n = POOL_PAD + tt
    return pl.pallas_call(
        functools.partial(_pool_body, tt=tt, pos0=pos0),
        grid=(B, T // tt),
        in_specs=[pl.BlockSpec((None, tt, C), lambda b, i: (b, i, 0)),
                  pl.BlockSpec((None, 16, C), lambda b, i: (b, 0, 0)),
                  pl.BlockSpec((None, tt, C), lambda b, i: (b, i, 0)),
                  _const_spec((C, C)), _const_spec((1, C))],
        out_specs=[pl.BlockSpec((None, tt, C), lambda b, i: (b, i, 0)),
                   pl.BlockSpec((None, 16, C), lambda b, i: (b, 0, 0))],
        out_shape=[jax.ShapeDtypeStruct((B, T, C), BF16), jax.ShapeDtypeStruct((B, 16, C), F32)],
        scratch_shapes=[pltpu.VMEM((n, C), F32)] * 4,
        compiler_params=_params("parallel", "arbitrary"),
        name="pool",
    )(pu, hist16, pz, w_bd, scale.reshape(1, C))


def _ssm_body(u_ref, bw_ref, cw_ref, are_ref, aim_ref, d_ref, wglu_ref, s0re_ref, s0im_ref,
              out_ref, sre_ref, sim_ref, u, bu_a, bu_b, y, *, nb, tt, cw):
    i = pl.program_id(0)

    @pl.when(i == 0)
    def _():
        sre_ref[...] = s0re_ref[...]
        sim_ref[...] = s0im_ref[...]

    for j in range(SSM_BLOCKS):
        for t in range(tt):
            u[t * nb:(t + 1) * nb, j * SSM_LANE_BLOCK:(j + 1) * SSM_LANE_BLOCK] = u_ref[j, pl.ds(t, nb, stride=tt), :]

    half = SSM_BLOCK_STATES
    for j in range(SSM_BLOCKS):
        lanes = slice(j * SSM_LANE_BLOCK, (j + 1) * SSM_LANE_BLOCK)
        bu = (bu_a, bu_b)[j % 2]
        bu[...] = jnp.dot(u[:, lanes].astype(BF16), bw_ref[j], preferred_element_type=F32)
        for c in range(half // cw):
            st = slice(j * half + c * cw, j * half + (c + 1) * cw)
            cre = slice(c * cw, (c + 1) * cw)
            cim = slice(half + c * cw, half + (c + 1) * cw)
            ar = jnp.broadcast_to(are_ref[:, st], (nb, cw))
            ai = jnp.broadcast_to(aim_ref[:, st], (nb, cw))
            hr, hi = sre_ref[:, st], sim_ref[:, st]
            for t in range(tt):
                rows = slice(t * nb, (t + 1) * nb)
                hr, hi = ar * hr - ai * hi + bu[rows, cre], ar * hi + ai * hr + bu[rows, cim]
                bu[rows, cre] = hr
                bu[rows, cim] = hi
            sre_ref[:, st] = hr
            sim_ref[:, st] = hi
        y[:, lanes] = jnp.dot(bu[...].astype(BF16), cw_ref[j], preferred_element_type=F32)

    yy = y[...] + d_ref[...] * u[...]
    act = jax.nn.gelu(yy).astype(BF16)
    zz = jnp.dot(act, wglu_ref[...], preferred_element_type=F32)
    y[...] = zz[:, :SSM_WIDTH] * jax.nn.sigmoid(zz[:, SSM_WIDTH:])
    for j in range(SSM_BLOCKS):
        for t in range(tt):
            out_ref[j, pl.ds(t, nb, stride=tt), :] = y[t * nb:(t + 1) * nb, j * SSM_LANE_BLOCK:(j + 1) * SSM_LANE_BLOCK]


def _ssm(su, nb, bw, cw_mat, a_re, a_im, d, w_glu, s0_re, s0_im):
    nt, nl, _, tt, lw = su.shape
    C = nl * lw
    rows = tt * nb
    cw = min(SSM_BLOCK_STATES, 128 * max(1, 64 // nb))
    tile = pl.BlockSpec((None, nl, rows, lw), lambda i: (i, 0, 0, 0))
    y, sre, sim = pl.pallas_call(
        functools.partial(_ssm_body, nb=nb, tt=tt, cw=cw),
        grid=(nt,),
        in_specs=[tile, _const_spec(bw.shape), _const_spec(cw_mat.shape),
                  _const_spec((1, SSM_STATES)), _const_spec((1, SSM_STATES)),
                  _const_spec((1, C)), _const_spec(w_glu.shape),
                  _const_spec((nb, SSM_STATES)), _const_spec((nb, SSM_STATES))],
        out_specs=[tile,
                   pl.BlockSpec((nb, SSM_STATES), lambda i: (0, 0)),
                   pl.BlockSpec((nb, SSM_STATES), lambda i: (0, 0))],
        out_shape=[jax.ShapeDtypeStruct((nt, nl, rows, lw), F32),
                   jax.ShapeDtypeStruct((nb, SSM_STATES), F32),
                   jax.ShapeDtypeStruct((nb, SSM_STATES), F32)],
        scratch_shapes=[pltpu.VMEM((rows, C), F32), pltpu.VMEM((rows, 2 * SSM_BLOCK_STATES), F32),
                        pltpu.VMEM((rows, 2 * SSM_BLOCK_STATES), F32), pltpu.VMEM((rows, C), F32)],
        compiler_params=_params("arbitrary"),
        name="ssm",
    )(su.reshape(nt, nl, rows, lw), bw, cw_mat, a_re.reshape(1, -1), a_im.reshape(1, -1), d.reshape(1, C), w_glu,
      s0_re, s0_im)
    return y.reshape(su.shape), sre, sim


def _ssm_weights(lam_re, lam_im, log_dt, b_re, b_im, c_re, c_im):
    dt = jnp.exp(log_dt.astype(F32))[:, None]
    lr, li = lam_re.astype(F32), lam_im.astype(F32)
    mag = jnp.exp(lr * dt)
    ab_re, ab_im = mag * jnp.cos(li * dt), mag * jnp.sin(li * dt)
    nr, ni = ab_re - 1.0, ab_im
    den = lr * lr + li * li
    f_re = (nr * lr + ni * li) / den
    f_im = (ni * lr - nr * li) / den
    br, bi = b_re.astype(F32), b_im.astype(F32)
    bb_re = f_re[..., None] * br - f_im[..., None] * bi
    bb_im = f_re[..., None] * bi + f_im[..., None] * br
    gpb = SSM_GROUPS // SSM_BLOCKS
    eye = jnp.eye(gpb, dtype=F32)

    def in_block(bb):
        bb = bb.reshape(SSM_BLOCKS, gpb, SSM_STATE, SSM_GROUP)
        m = jnp.einsum('jgpc,gh->jgchp', bb, eye)
        return m.reshape(SSM_BLOCKS, gpb * SSM_GROUP, gpb * SSM_STATE)

    def out_block(cc):
        cc = cc.reshape(SSM_BLOCKS, gpb, SSM_GROUP, SSM_STATE)
        m = jnp.einsum('jgcp,gh->jgphc', cc, eye)
        return m.reshape(SSM_BLOCKS, gpb * SSM_STATE, gpb * SSM_GROUP)

    bw = jnp.concatenate([in_block(bb_re), in_block(bb_im)], axis=-1).astype(BF16)
    cwm = jnp.concatenate([out_block(c_re.astype(F32)), -out_block(c_im.astype(F32))], axis=1).astype(BF16)
    return bw, cwm, ab_re.reshape(-1), ab_im.reshape(-1)


def _split_q(q):
    qf = q.astype(F32) * (ATTN_HEAD_DIM ** -0.5)
    lane = lax.broadcasted_iota(jnp.int32, (1, ATTN_V_DIM), 1)
    first = lane < ATTN_HEAD_DIM
    return jnp.where(first, qf, 0.0).astype(BF16), jnp.where(first, 0.0, qf).astype(BF16)


def _scores(q, k):
    return lax.dot_general(q, k, (((1,), (1,)), ((), ())), preferred_element_type=F32)


def _online(t, v, m, l, acc):
    m_new = jnp.maximum(m, jnp.max(t, axis=-1, keepdims=True))
    alpha = jnp.exp(m - m_new)
    p = jnp.exp(t - m_new)
    l_new = alpha * l + jnp.sum(p, axis=-1, keepdims=True)
    acc_new = alpha * acc + jnp.dot(p.astype(BF16), v, preferred_element_type=F32)
    return m_new, l_new, acc_new


def _attn_finish(o1, o2, lq1, lk1, lq2, lk2, subg, az, lambda_init):
    lam = (jnp.exp(jnp.sum(lq1 * lk1, axis=-1, keepdims=True))
           - jnp.exp(jnp.sum(lq2 * lk2, axis=-1, keepdims=True)) + lambda_init)
    o = o1 - lam * o2
    o = _rms_rows(o, subg) * (1.0 - lambda_init)
    return o * jax.nn.silu(az.astype(F32))


ATTN_ONES_ROWS = 16
ATTN_HEAD_GROUP = 4


def _online_t(t, vx, m, acc, shift):
    m_new = jnp.maximum(m, jnp.max(t, axis=0, keepdims=True) + shift)
    alpha = jnp.exp(m - m_new)
    p = jnp.exp(t + (shift - m_new))
    return m_new, alpha * acc + jnp.dot(vx, p.astype(BF16), preferred_element_type=F32)


def _attn_prompt_body(slope_ref, lq1_ref, lk1_ref, lq2_ref, lk2_ref, subg_ref,
                      q_ref, k_ref, v_ref, az_ref, out_ref, kx, vxt, corr, wq, *state,
                      tq, nk, heads, lambda_init):
    qi = pl.program_id(2)
    hd = ATTN_V_DIM
    rows_acc = hd + ATTN_ONES_ROWS
    lane = lax.broadcasted_iota(jnp.int32, (tq, hd), 1)
    row = lax.broadcasted_iota(jnp.int32, (tq, hd), 0).astype(F32)
    slopes = [slope_ref[g, 0:1, 0:1] for g in range(heads)]
    cols = [slice(g * hd, (g + 1) * hd) for g in range(heads)]
    nc = 2 * heads
    mst, tmax_a, tmax_b, ast, t_a, t_b = (state[n * nc:(n + 1) * nc] for n in range(6))
    tmax_s, t_s = (tmax_a, tmax_b), (t_a, t_b)

    @pl.when(qi == 0)
    def _():
        ones = jnp.ones((ATTN_ONES_ROWS, tq), BF16)
        c = lax.broadcasted_iota(jnp.int32, (tq, tq), 0)
        r = lax.broadcasted_iota(jnp.int32, (tq, tq), 1)
        visible = (c >> 6) <= (r >> 6)
        ahead = jnp.where(c > r, (c - r).astype(F32), 0.0)
        for g in range(heads):
            kfeat = jnp.where(lane == 0, row, jnp.where(lane == 1, -slopes[g], 0.0)).astype(BF16)
            for j in range(nk):
                rows = slice(j * tq, (j + 1) * tq)
                kx[g, rows, 0:hd] = k_ref[rows, cols[g]]
                kx[g, rows, hd:2 * hd] = kfeat
                vxt[g, j, 0:hd, :] = v_ref[rows, cols[g]].astype(F32).T.astype(BF16)
                vxt[g, j, hd:rows_acc, :] = ones
            corr[g] = jnp.where(visible, -2.0 * slopes[g] * ahead, NEG_INF)

    first = lane < ATTN_HEAD_DIM
    for g in range(heads):
        qf = q_ref[:, cols[g]].astype(F32) * (ATTN_HEAD_DIM ** -0.5)
        qfeat = jnp.where(lane == 0, slopes[g], jnp.where(lane == 1, row, 0.0))
        wq[2 * g] = jnp.concatenate([jnp.where(first, qf, 0.0), qfeat], axis=1).T.astype(BF16)
        wq[2 * g + 1] = jnp.concatenate([jnp.where(first, 0.0, qf), qfeat], axis=1).T.astype(BF16)
    chains = [(g, 2 * g + n) for g in range(heads) for n in range(2)]
    for _, c in chains:
        mst[c][...] = jnp.full((1, tq), NEG_INF, F32)
        ast[c][...] = jnp.zeros((rows_acc, tq), F32)

    def scores(j, buf):
        kstart = pl.multiple_of(j * tq, tq)
        for g, c in chains:
            t = jnp.dot(kx[g, pl.ds(kstart, tq), :], wq[c], preferred_element_type=F32)
            t_s[buf][c][...] = t
            tmax_s[buf][c][...] = jnp.max(t, axis=0, keepdims=True)

    def update(j, buf, diagonal=False):
        for g, c in chains:
            if diagonal:
                t = t_s[buf][c][...] + corr[g]
                tmax = jnp.max(t, axis=0, keepdims=True)
                shift = jnp.zeros((1, 1), F32)
            else:
                t, tmax = t_s[buf][c][...], tmax_s[buf][c][...]
                shift = -slopes[g] * ((qi - j) * tq).astype(F32)
            m_old = mst[c][...]
            m_new = jnp.maximum(m_old, tmax + shift)
            mst[c][...] = m_new
            p = jnp.exp(t + (shift - m_new)).astype(BF16)
            ast[c][...] = jnp.exp(m_old - m_new) * ast[c][...] + jnp.dot(vxt[g, j], p,
                                                                         preferred_element_type=F32)

    def step(j, buf):
        scores(j + 1, 1 - buf)
        update(j, buf)

    def pair(k, carry):
        step(2 * k, 0)
        step(2 * k + 1, 1)
        return carry

    scores(0, 0)
    lax.fori_loop(0, qi // 2, pair, 0)

    @pl.when(qi % 2 == 1)
    def _():
        step(qi - 1, 0)
        update(qi, 1, diagonal=True)

    @pl.when(qi % 2 == 0)
    def _():
        update(qi, 0, diagonal=True)

    lam = (jnp.exp(jnp.sum(lq1_ref[...] * lk1_ref[...], axis=-1, keepdims=True))
           - jnp.exp(jnp.sum(lq2_ref[...] * lk2_ref[...], axis=-1, keepdims=True)) + lambda_init)
    for g in range(heads):
        a1, a2 = ast[2 * g][...], ast[2 * g + 1][...]
        o = (a1[0:hd, :] / a1[hd:hd + 1, :] - lam * (a2[0:hd, :] / a2[hd:hd + 1, :])).T
        o = _rms_rows(o, subg_ref[...]) * (1.0 - lambda_init)
        out_ref[:, cols[g]] = (o * jax.nn.silu(az_ref[:, cols[g]].astype(F32))).astype(out_ref.dtype)


def _attn_small_inputs(slopes, lq1, lk1, lq2, lk2, subg):
    slope_rows = jnp.broadcast_to(slopes[:, None, None], (ATTN_HEADS, 1, 128))
    r = lambda a: a.astype(F32).reshape(1, -1)
    return slope_rows, r(lq1), r(lk1), r(lq2), r(lk2), r(subg)


def _small_specs():
    return [pl.BlockSpec((None, 1, 128), lambda b, h, *_: (h, 0, 0)),
            _const_spec((1, ATTN_HEAD_DIM)), _const_spec((1, ATTN_HEAD_DIM)),
            _const_spec((1, ATTN_HEAD_DIM)), _const_spec((1, ATTN_HEAD_DIM)),
            _const_spec((1, ATTN_V_DIM))]


def _attn_prompt(q, k, v, az, small, lambda_init):
    B, T, W = q.shape
    tq = _row_tile(T, 256)
    assert tq % CHUNK == 0 and tq <= 256
    nk = T // tq
    hd = ATTN_V_DIM
    hg = ATTN_HEAD_GROUP
    rows_acc = hd + ATTN_ONES_ROWS
    tile = pl.BlockSpec((None, tq, hg * hd), lambda b, h, i: (b, i, h))
    full = pl.BlockSpec((None, T, hg * hd), lambda b, h, i: (b, 0, h))
    small_specs = [pl.BlockSpec((hg, 1, 128), lambda b, h, i: (h, 0, 0))] + _small_specs()[1:]
    return pl.pallas_call(
        functools.partial(_attn_prompt_body, tq=tq, nk=nk, heads=hg, lambda_init=lambda_init),
        grid=(B, ATTN_HEADS // hg, nk),
        in_specs=small_specs + [tile, full, full, tile],
        out_specs=tile,
        out_shape=jax.ShapeDtypeStruct((B, T, W), BF16),
        scratch_shapes=[pltpu.VMEM((hg, T, 2 * hd), BF16),
                        pltpu.VMEM((hg, nk, rows_acc, tq), BF16),
                        pltpu.VMEM((hg, tq, tq), F32),
                        pltpu.VMEM((2 * hg, 2 * hd, tq), BF16)]
        + [pltpu.VMEM((1, tq), F32)] * (3 * 2 * hg) + [pltpu.VMEM((rows_acc, tq), F32)] * (2 * hg)
        + [pltpu.VMEM((tq, tq), F32)] * (2 * 2 * hg),
        compiler_params=_params("parallel", "parallel", "arbitrary"),
        name="attn_prompt",
    )(*small, q, k, v, az)


def _attn_sample_body(slope_ref, lq1_ref, lk1_ref, lq2_ref, lk2_ref, subg_ref,
                      q_ref, kp_ref, vp_ref, kn_ref, vn_ref, az_ref, out_ref, m_s, l_s, acc_s,
                      *, past, pc, lambda_init):
    j = pl.program_id(1)
    T = q_ref.shape[0]
    hd = ATTN_V_DIM
    heads = ATTN_HEADS
    cols = [slice(h * hd, (h + 1) * hd) for h in range(heads)]
    slopes = [slope_ref[h, 0:1, 0:1] for h in range(heads)]
    qpos = past + (lax.broadcasted_iota(jnp.int32, (2 * T, 1), 0) & (T - 1))
    qposf = qpos.astype(F32)

    @pl.when(j == 0)
    def _():
        m_s[...] = jnp.full(m_s.shape, NEG_INF, F32)
        l_s[...] = jnp.zeros(l_s.shape, F32)
        acc_s[...] = jnp.zeros(acc_s.shape, F32)

    qs = [jnp.concatenate(_split_q(q_ref[:, cols[h]]), axis=0) for h in range(heads)]

    def attend(keys, vals, bias_of):
        ss = [_scores(qs[h], keys[h]) + bias_of(h) for h in range(heads)]
        for h in range(heads):
            m_s[h], l_s[h], acc_s[h] = _online(ss[h], vals[h], m_s[h], l_s[h], acc_s[h])

    kposf = (j * pc + lax.broadcasted_iota(jnp.int32, (1, pc), 1)).astype(F32)
    attend([kp_ref[pl.ds(h, pc, stride=heads), :].astype(BF16) for h in range(heads)],
           [vp_ref[pl.ds(h, pc, stride=heads), :].astype(BF16) for h in range(heads)],
           lambda h: slopes[h] * kposf - slopes[h] * qposf)

    @pl.when(j == pl.num_programs(1) - 1)
    def _():
        kpos = past + lax.broadcasted_iota(jnp.int32, (1, T), 1)
        dist = jnp.abs(qpos - kpos).astype(F32)
        visible = (kpos >> 6) <= (qpos >> 6)
        attend([kn_ref[:, cols[h]].astype(BF16) for h in range(heads)],
               [vn_ref[:, cols[h]].astype(BF16) for h in range(heads)],
               lambda h: jnp.where(visible, -slopes[h] * dist, NEG_INF))
        for h in range(heads):
            o = acc_s[h] / l_s[h]
            o = _attn_finish(o[0:T], o[T:2 * T], lq1_ref[...], lk1_ref[...], lq2_ref[...], lk2_ref[...],
                             subg_ref[...], az_ref[:, cols[h]], lambda_init)
            out_ref[:, cols[h]] = o.astype(out_ref.dtype)


def _attn_sample(q, k_cache, v_cache, k_stack, v_stack, layer, az, small, lambda_init):
    B, T, W = q.shape
    depth, _, P = k_cache.shape[:3]
    assert P % CHUNK == 0 and T == CHUNK and P > 0
    hd = ATTN_V_DIM
    pc = _row_tile(P, 1024)
    tile = pl.BlockSpec((None, T, W), lambda b, j: (b, 0, 0))
    new = pl.BlockSpec((None, None, T, W), lambda b, j: (layer, b, 0, 0))
    pastb = pl.BlockSpec((None, None, pc * ATTN_HEADS, hd), lambda b, j: (layer, b, j, 0))
    small_specs = [pl.BlockSpec((ATTN_HEADS, 1, 128), lambda b, j: (0, 0, 0))] + _small_specs()[1:]
    rows = lambda a: a.reshape(depth, B, P * ATTN_HEADS, hd)
    return pl.pallas_call(
        functools.partial(_attn_sample_body, past=P, pc=pc, lambda_init=lambda_init),
        grid=(B, P // pc),
        in_specs=small_specs + [tile, pastb, pastb, new, new, tile],
        out_specs=tile,
        out_shape=jax.ShapeDtypeStruct((B, T, W), BF16),
        scratch_shapes=[pltpu.VMEM((ATTN_HEADS, 2 * T, 1), F32), pltpu.VMEM((ATTN_HEADS, 2 * T, 1), F32),
                        pltpu.VMEM((ATTN_HEADS, 2 * T, hd), F32)],
        compiler_params=_params("parallel", "arbitrary"),
        name="attn_sample",
    )(*small, q, rows(k_cache), rows(v_cache), k_stack, v_stack, az)


def _memat_body(q_ref, k_ref, v_ref, out_ref):
    q = q_ref[...]
    k = k_ref[...]
    v = v_ref[...]
    lane = lax.broadcasted_iota(jnp.int32, (1, MEM_WIDTH), 1)
    acc = jnp.zeros(out_ref.shape, F32)
    for h in range(MEM_HEADS):
        mine = (lane >= h * MEM_HEAD_DIM) & (lane < (h + 1) * MEM_HEAD_DIM)
        kh = jnp.where(mine, k, 0.0).astype(BF16)
        vh = jnp.where(mine, v, 0.0).astype(BF16)
        s = _scores(q, kh) * (MEM_HEAD_DIM ** -0.5)
        p = jnp.exp(s - jnp.max(s, axis=-1, keepdims=True))
        a = p / jnp.sum(p, axis=-1, keepdims=True)
        acc = acc + jnp.dot(a.astype(BF16), vh, preferred_element_type=F32)
    out_ref[...] = acc.astype(out_ref.dtype)


def _memat(mq, mk, mv):
    B, T, W = mq.shape
    M = mk.shape[1]
    tm = _row_tile(T, 512)
    return pl.pallas_call(
        _memat_body,
        grid=(B, T // tm),
        in_specs=[pl.BlockSpec((None, tm, W), lambda b, i: (b, i, 0)),
                  pl.BlockSpec((None, M, W), lambda b, i: (b, 0, 0)),
                  pl.BlockSpec((None, M, W), lambda b, i: (b, 0, 0))],
        out_specs=pl.BlockSpec((None, tm, W), lambda b, i: (b, i, 0)),
        out_shape=jax.ShapeDtypeStruct((B, T, W), BF16),
        compiler_params=_params("parallel", "parallel"),
        name="memat",
    )(mq, mk, mv)


def _merge_body(x_ref, g_ref, pool_ref, ssm_ref, sz_ref, attn_ref, mem_ref,
                wp_ref, ws_ref, wa_ref, wm_ref, wg_ref, wo_ref, fg_ref, out_ref, *, final_norm):
    x = x_ref[...]
    D = x.shape[-1]
    h = _rms_rows(x, g_ref[...]).astype(BF16)
    rows = sz_ref.shape[0]
    ssm_y = jnp.concatenate([ssm_ref[:, j].reshape(rows, -1) for j in range(ssm_ref.shape[1])], axis=1)
    ssm_g = (ssm_y * jax.nn.silu(sz_ref[...].astype(F32))).astype(BF16)
    merged = jnp.zeros(x.shape, F32)
    branches = ((pool_ref[...], wp_ref), (ssm_g, ws_ref), (attn_ref[...], wa_ref), (mem_ref[...], wm_ref))
    for n, (a, w_ref) in enumerate(branches):
        gate = jax.nn.sigmoid(jnp.dot(h, wg_ref[:, n * D:(n + 1) * D], preferred_element_type=F32))
        merged = merged + gate * jnp.dot(a, w_ref[...], preferred_element_type=F32)
    y = x + jnp.dot(merged.astype(BF16), wo_ref[...], preferred_element_type=F32)
    if final_norm:
        y = _rms_rows(y, fg_ref[...])
    out_ref[...] = y


def _merge(x, g, pool_g, ssm_y, sz, attn_g, mem_o, wp, ws, wa, wm, wg, wo, fg, final_norm):
    B, T, D = x.shape
    tm = _row_tile(T, 512)
    tt = ssm_y.shape[3]
    bt = lambda wd: pl.BlockSpec((None, tm, wd), lambda b, i: (b, i, 0))
    return pl.pallas_call(
        functools.partial(_merge_body, final_norm=final_norm),
        grid=(B, T // tm),
        in_specs=[bt(D), _const_spec((1, D)), bt(POOL_WIDTH),
                  pl.BlockSpec((tm // tt, SSM_BLOCKS, None, tt, SSM_LANE_BLOCK), lambda b, i: (i, 0, b, 0, 0)),
                  bt(SSM_WIDTH),
                  bt(ATTN_WIDTH), bt(MEM_WIDTH),
                  _const_spec(wp.shape), _const_spec(ws.shape), _const_spec(wa.shape), _const_spec(wm.shape),
                  _const_spec(wg.shape), _const_spec(wo.shape), _const_spec((1, D))],
        out_specs=bt(D),
        out_shape=jax.ShapeDtypeStruct((B, T, D), F32),
        compiler_params=_params("parallel", "parallel"),
        name="merge",
    )(x, g.reshape(1, D), pool_g, ssm_y, sz, attn_g, mem_o, wp, ws, wa, wm, wg, wo, fg.reshape(1, D))


def _trunk_layer(x, layer, depth, pool_hist, ssm_re, ssm_im, caches, kv_stacks, mem_k, mem_v, lw, lambda_init,
                 final_g):
    B, T, D = x.shape
    pu, pz, su, sz, q, k_stack, k16, v_stack, v16, az, mq = _trunk_proj(
        x, lw['norm_g'], lw['w_in_main'], layer, depth, kv_stacks)
    pos0 = 0 if caches is None else caches[0].shape[2]
    hist16 = jnp.concatenate([jnp.zeros((B, 1, POOL_WIDTH), F32), pool_hist], axis=1)
    pool_g, hist_new = _pool(pu, hist16, pz, lw['pool_w_bd'], lw['pool_scale'], pos0)
    ssm_y, sre, sim = _ssm(su, B, lw['ssm_bw'], lw['ssm_cw'], lw['ssm_a_re'], lw['ssm_a_im'], lw['ssm_d'],
                           lw['w_glu'], ssm_re.reshape(B, SSM_STATES), ssm_im.reshape(B, SSM_STATES))
    if caches is None:
        attn_g = _attn_prompt(q, k16, v16, az, lw['attn_small'], lambda_init)
    else:
        attn_g = _attn_sample(q, caches[0], caches[1], k_stack, v_stack, layer, az, lw['attn_small'], lambda_init)
    mem_o = _memat(mq, mem_k, mem_v)
    y = _merge(x, lw['norm_g'], pool_g, ssm_y, sz, attn_g, mem_o,
               lw['w_br_pool'], lw['w_br_ssm'], lw['w_br_attn'], lw['w_br_mem'], lw['w_in_gates'], lw['w_out'],
               final_g, layer == depth - 1)
    return (y, hist_new[:, 1:], sre.reshape(B, SSM_GROUPS, SSM_STATE), sim.reshape(B, SSM_GROUPS, SSM_STATE),
            (k_stack, v_stack))


def kernel(x_prompt, x_sample, mem_prompt, cache_attn_k, cache_attn_v, cache_mem_k, cache_mem_v, state_pool, state_ssm_re, state_ssm_im, norm_g, w_in, pool_w, pool_scale, ssm_lambda_re, ssm_lambda_im, ssm_log_dt, ssm_b_re, ssm_b_im, ssm_c_re, ssm_c_im, ssm_d, ssm_w_glu, attn_lq1, attn_lk1, attn_lq2, attn_lk2, attn_subln_g, mem_norm_g, w_mem_kv, w_br_pool, w_br_ssm, w_br_attn, w_br_mem, w_out, final_norm_g):
    depth = w_in.shape[0]
    B = x_prompt.shape[0]
    Bd = x_sample.shape[0]
    n_main = w_in.shape[-1] - N_BRANCH * x_prompt.shape[-1]
    slopes = jnp.asarray([2.0 ** (-8.0 * (h + 1) / ATTN_HEADS) for h in range(ATTN_HEADS)], F32)
    xp, xs = x_prompt, x_sample
    outs = {n: [] for n in ('mkp', 'mvp', 'poolp', 'srep', 'simp', 'pools', 'sres', 'sims')}
    kv_p = kv_s = None
    for l in range(depth):
        bw, cwm, a_re, a_im = _ssm_weights(ssm_lambda_re[l], ssm_lambda_im[l], ssm_log_dt[l],
                                           ssm_b_re[l], ssm_b_im[l], ssm_c_re[l], ssm_c_im[l])
        eye = jnp.eye(len(POOL_WINDOWS), dtype=F32)
        pool_w_bd = jnp.einsum('gcd,gh->gchd', pool_w[l].astype(F32), eye).reshape(POOL_WIDTH, POOL_WIDTH)
        lw = {
            'norm_g': norm_g[l], 'w_in_main': w_in[l, :, :n_main].astype(BF16),
            'w_in_gates': w_in[l, :, n_main:].astype(BF16),
            'pool_w_bd': pool_w_bd.astype(BF16), 'pool_scale': pool_scale[l],
            'ssm_bw': bw, 'ssm_cw': cwm, 'ssm_a_re': a_re, 'ssm_a_im': a_im, 'ssm_d': ssm_d[l],
            'w_glu': ssm_w_glu[l].astype(BF16),
            'attn_small': _attn_small_inputs(slopes, attn_lq1[l], attn_lk1[l], attn_lq2[l], attn_lk2[l],
                                             attn_subln_g[l]),
            'w_br_pool': w_br_pool[l].astype(BF16), 'w_br_ssm': w_br_ssm[l].astype(BF16),
            'w_br_attn': w_br_attn[l].astype(BF16), 'w_br_mem': w_br_mem[l].astype(BF16),
            'w_out': w_out[l].astype(BF16),
        }
        lambda_init = 0.8 - 0.6 * math.exp(-0.3 * l)
        mk_p, mv_p = _memory_kv(mem_prompt, mem_norm_g[l], w_mem_kv[l].astype(BF16))
        zero_pool = jnp.zeros((B, POOL_HIST, POOL_WIDTH), F32)
        zero_ssm = jnp.zeros((B, SSM_GROUPS, SSM_STATE), F32)
        xp, pool_p, sre_p, sim_p, kv_p = _trunk_layer(
            xp, l, depth, zero_pool, zero_ssm, zero_ssm, None, kv_p, mk_p, mv_p, lw, lambda_init, final_norm_g)
        M = cache_mem_k.shape[2]
        xs, pool_s, sre_s, sim_s, kv_s = _trunk_layer(
            xs, l, depth, state_pool[l], state_ssm_re[l], state_ssm_im[l], (cache_attn_k, cache_attn_v), kv_s,
            cache_mem_k[l].reshape(Bd, M, MEM_WIDTH), cache_mem_v[l].reshape(Bd, M, MEM_WIDTH),
            lw, lambda_init, final_norm_g)
        mshape = (B, mem_prompt.shape[1], MEM_HEADS, MEM_HEAD_DIM)
        for n, a in (('mkp', mk_p.reshape(mshape)), ('mvp', mv_p.reshape(mshape)),
                     ('poolp', pool_p), ('srep', sre_p), ('simp', sim_p),
                     ('pools', pool_s), ('sres', sre_s), ('sims', sim_s)):
            outs[n].append(a)
    st = {n: jnp.stack(a) for n, a in outs.items()}
    heads = lambda a: a.reshape(a.shape[:3] + (ATTN_HEADS, ATTN_V_DIM))
    return (xp, xs, heads(kv_p[0]), heads(kv_p[1]), st['mkp'], st['mvp'], st['poolp'], st['srep'], st['simp'],
            heads(kv_s[0]), heads(kv_s[1]), st['pools'], st['sres'], st['sims'])
```

```python
import functools
import math
import struct

import jax
import jax.numpy as jnp
from jax import lax
from jax.experimental import pallas as pl
from jax.experimental.pallas import tpu as pltpu

F32 = jnp.float32
BF16 = jnp.bfloat16

CHUNK = 64
EPS = 1e-6
NEG_INF = -1e30

POOL_WIDTH = 768
POOL_WINDOWS = (2, 4, 8, 16)
POOL_GROUP = POOL_WIDTH // len(POOL_WINDOWS)
POOL_HIST = 15
POOL_PAD = 32

SSM_WIDTH = 768
SSM_GROUP = 16
SSM_GROUPS = SSM_WIDTH // SSM_GROUP
SSM_STATE = 64
SSM_STATES = SSM_GROUPS * SSM_STATE
SSM_LANE_BLOCK = 128
SSM_BLOCKS = SSM_WIDTH // SSM_LANE_BLOCK
SSM_BLOCK_STATES = SSM_STATES // SSM_BLOCKS

ATTN_HEADS = 8
ATTN_HEAD_DIM = 64
ATTN_V_DIM = 128
ATTN_WIDTH = ATTN_HEADS * ATTN_V_DIM

MEM_HEADS = 4
MEM_HEAD_DIM = 64
MEM_WIDTH = MEM_HEADS * MEM_HEAD_DIM

N_BRANCH = 4

VMEM_LIMIT_BYTES = 56 * 1024 * 1024


def _params(*sem):
    return pltpu.CompilerParams(dimension_semantics=sem, vmem_limit_bytes=VMEM_LIMIT_BYTES)


def _const_spec(shape):
    nd = len(shape)
    return pl.BlockSpec(shape, lambda *_: (0,) * nd, pipeline_mode=pl.Buffered(1))


def _rms_rows(x, g):
    return x * lax.rsqrt(jnp.mean(x * x, axis=-1, keepdims=True) + EPS) * g


def _proj_body(x_ref, g_ref, w_ref, *out_refs, segs):
    h = _rms_rows(x_ref[...], g_ref[...]).astype(BF16)
    outs = iter(out_refs)
    for a, b, modes in segs:
        y = jnp.dot(h, w_ref[:, a:b], preferred_element_type=F32)
        for mode in modes:
            o_ref = next(outs)
            if mode == 'slots':
                for s in range(o_ref.shape[0]):
                    o_ref[s] = y.astype(o_ref.dtype)
            elif mode == 'tiles':
                nt, nl, tt, lw = o_ref.shape
                for j in range(nl):
                    o_ref[:, j] = y[:, j * lw:(j + 1) * lw].astype(o_ref.dtype).reshape(nt, tt, lw)
            elif mode == 'scaled_q':
                o_ref[...] = (y * ATTN_Q_SCALE).astype(o_ref.dtype)
            else:
                o_ref[...] = y.astype(o_ref.dtype)


def _row_tile(t, want):
    tm = min(t, want)
    assert t % tm == 0
    return tm


def _proj(x, g, w, segs, out_shapes, out_specs, tm, carried=None):
    B, T, D = x.shape
    carried = carried or {}

    def body(x_ref, g_ref, w_ref, *refs):
        _proj_body(x_ref, g_ref, w_ref, *refs[len(carried):], segs=segs)

    return pl.pallas_call(
        body,
        grid=(B, T // tm),
        in_specs=[pl.BlockSpec((None, tm, D), lambda b, i: (b, i, 0)),
                  _const_spec((1, D)), _const_spec(w.shape)]
        + [pl.BlockSpec(memory_space=pl.ANY)] * len(carried),
        out_specs=out_specs,
        out_shape=out_shapes,
        input_output_aliases={3 + n: o for n, o in enumerate(carried)},
        compiler_params=_params("parallel", "parallel"),
        name="proj",
    )(x, g.reshape(1, D), w, *carried.values())


def _ssm_time_tile(T, nb):
    return _row_tile(T, max(8, 1024 // nb))


def _trunk_proj(x, g, w, layer, depth, kv_stacks, scaled_q):
    B, T, D = x.shape
    tm = _row_tile(T, 512)
    tt = _ssm_time_tile(T, B)
    bounds = [0, 768, 1536, 2304, 3072, 4096, 5120, 6144, 7168, 7424]
    first = kv_stacks is None
    stack_mode = 'slots' if first else 'rows'
    q_mode = 'scaled_q' if scaled_q else 'rows'
    modes = [('rows',), ('rows',), ('tiles',), ('rows',), (q_mode,), (stack_mode, 'rows'), (stack_mode, 'rows'),
             ('rows',), ('rows',)]
    segs = tuple(zip(bounds[:-1], bounds[1:], modes))
    bt = lambda wd, dt: jax.ShapeDtypeStruct((B, T, wd), dt)
    s_bt = lambda wd: pl.BlockSpec((None, tm, wd), lambda b, i: (b, i, 0))
    tiles = jax.ShapeDtypeStruct((T // tt, SSM_BLOCKS, B, tt, SSM_LANE_BLOCK), F32)
    s_tiles = pl.BlockSpec((tm // tt, SSM_BLOCKS, None, tt, SSM_LANE_BLOCK), lambda b, i: (i, 0, b, 0, 0))
    stack = jax.ShapeDtypeStruct((depth, B, T, ATTN_WIDTH), F32)
    if first:
        s_stack = pl.BlockSpec((depth, None, tm, ATTN_WIDTH), lambda b, i: (0, b, i, 0))
    else:
        s_stack = pl.BlockSpec((None, None, tm, ATTN_WIDTH), lambda b, i: (layer, b, i, 0))
    shapes = [bt(768, F32), bt(768, BF16), tiles, bt(768, BF16), bt(1024, BF16),
              stack, bt(1024, BF16), stack, bt(1024, BF16), bt(1024, BF16), bt(256, BF16)]
    specs = [s_bt(768), s_bt(768), s_tiles, s_bt(768), s_bt(1024),
             s_stack, s_bt(1024), s_stack, s_bt(1024), s_bt(1024), s_bt(256)]
    carried = None if first else {5: kv_stacks[0], 7: kv_stacks[1]}
    return _proj(x, g, w, segs, shapes, specs, tm, carried)


def _memory_kv(mem, g, w):
    B, M, D = mem.shape
    segs = ((0, MEM_WIDTH, ('rows',)), (MEM_WIDTH, 2 * MEM_WIDTH, ('rows',)))
    shapes = [jax.ShapeDtypeStruct((B, M, MEM_WIDTH), F32)] * 2
    specs = [pl.BlockSpec((None, M, MEM_WIDTH), lambda b, i: (b, i, 0))] * 2
    return _proj(mem, g, w, segs, shapes, specs, M)


def _pool_body(u_ref, hist_ref, z_ref, w_ref, scale_ref, out_ref, hist_out_ref,
               ext, b2, b4, b8, *, tt, pos0):
    i = pl.program_id(1)
    n = POOL_PAD + tt

    @pl.when(i == 0)
    def _():
        ext[0:16, :] = jnp.zeros((16, POOL_WIDTH), F32)
        ext[16:32, :] = hist_ref[...]

    @pl.when(i > 0)
    def _():
        ext[16:32, :] = ext[tt + 16:tt + 32, :]

    u = u_ref[...]
    ext[32:n, :] = u
    b2[8:n, :] = ext[8:n, :] + ext[7:n - 1, :]
    b4[16:n, :] = b2[16:n, :] + b2[14:n - 2, :]
    b8[24:n, :] = b4[24:n, :] + b4[20:n - 4, :]
    s2 = b2[32:n, :]
    s4 = b4[32:n, :]
    s8 = b8[32:n, :]
    s16 = s8 + b8[24:n - 8, :]
    col = lax.broadcasted_iota(jnp.int32, (1, POOL_WIDTH), 1)
    wsum = jnp.where(col < POOL_GROUP, s2,
                     jnp.where(col < 2 * POOL_GROUP, s4, jnp.where(col < 3 * POOL_GROUP, s8, s16)))
    win = jnp.where(col < POOL_GROUP, 2, jnp.where(col < 2 * POOL_GROUP, 4, jnp.where(col < 3 * POOL_GROUP, 8, 16)))
    pos = pos0 + i * tt + lax.broadcasted_iota(jnp.int32, (tt, 1), 0)
    cnt = jnp.minimum(win, pos + 1).astype(F32)
    diff = (wsum / cnt - u).astype(BF16)
    y = jnp.dot(diff, w_ref[...], preferred_element_type=F32) * scale_ref[...]
    out_ref[...] = (y * jax.nn.silu(z_ref[...].astype(F32))).astype(out_ref.dtype)
    hist_out_ref[...] = ext[tt + 16:tt + 32, :]


def _pool(pu, hist16, pz, w_bd, scale, pos0):
    B, T, C = pu.shape
    tt = _row_tile(T, 512)
    n = POOL_PAD + tt
    return pl.pallas_call(
        functools.partial(_pool_body, tt=tt, pos0=pos0),
        grid=(B, T // tt),
        in_specs=[pl.BlockSpec((None, tt, C), lambda b, i: (b, i, 0)),
                  pl.BlockSpec((None, 16, C), lambda b, i: (b, 0, 0)),
                  pl.BlockSpec((None, tt, C), lambda b, i: (b, i, 0)),
                  _const_spec((C, C)), _const_spec((1, C))],
        out_specs=[pl.BlockSpec((None, tt, C), lambda b, i: (b, i, 0)),
                   pl.BlockSpec((None, 16, C), lambda b, i: (b, 0, 0))],
        out_shape=[jax.ShapeDtypeStruct((B, T, C), BF16), jax.ShapeDtypeStruct((B, 16, C), F32)],
        scratch_shapes=[pltpu.VMEM((n, C), F32)] * 4,
        compiler_params=_params("parallel", "arbitrary"),
        name="pool",
    )(pu, hist16, pz, w_bd, scale.reshape(1, C))


def _ssm_body(u_ref, bw_ref, cw_ref, are_ref, aim_ref, d_ref, wglu_ref, s0re_ref, s0im_ref,
              out_ref, sre_ref, sim_ref, u, bu_a, bu_b, y, *, nb, tt, cw):
    i = pl.program_id(0)

    @pl.when(i == 0)
    def _():
        sre_ref[...] = s0re_ref[...]
        sim_ref[...] = s0im_ref[...]

    for j in range(SSM_BLOCKS):
        for t in range(tt):
            u[t * nb:(t + 1) * nb, j * SSM_LANE_BLOCK:(j + 1) * SSM_LANE_BLOCK] = u_ref[j, pl.ds(t, nb, stride=tt), :]

    half = SSM_BLOCK_STATES
    for j in range(SSM_BLOCKS):
        lanes = slice(j * SSM_LANE_BLOCK, (j + 1) * SSM_LANE_BLOCK)
        bu = (bu_a, bu_b)[j % 2]
        bu[...] = jnp.dot(u[:, lanes].astype(BF16), bw_ref[j], preferred_element_type=F32)
        for c in range(half // cw):
            st = slice(j * half + c * cw, j * half + (c + 1) * cw)
            cre = slice(c * cw, (c + 1) * cw)
            cim = slice(half + c * cw, half + (c + 1) * cw)
            ar = jnp.broadcast_to(are_ref[:, st], (nb, cw))
            ai = jnp.broadcast_to(aim_ref[:, st], (nb, cw))
            hr, hi = sre_ref[:, st], sim_ref[:, st]
            for t in range(tt):
                rows = slice(t * nb, (t + 1) * nb)
                hr, hi = ar * hr - ai * hi + bu[rows, cre], ar * hi + ai * hr + bu[rows, cim]
                bu[rows, cre] = hr
                bu[rows, cim] = hi
            sre_ref[:, st] = hr
            sim_ref[:, st] = hi
        y[:, lanes] = jnp.dot(bu[...].astype(BF16), cw_ref[j], preferred_element_type=F32)

    yy = y[...] + d_ref[...] * u[...]
    act = jax.nn.gelu(yy).astype(BF16)
    zz = jnp.dot(act, wglu_ref[...], preferred_element_type=F32)
    y[...] = zz[:, :SSM_WIDTH] * jax.nn.sigmoid(zz[:, SSM_WIDTH:])
    for j in range(SSM_BLOCKS):
        for t in range(tt):
            out_ref[j, pl.ds(t, nb, stride=tt), :] = y[t * nb:(t + 1) * nb, j * SSM_LANE_BLOCK:(j + 1) * SSM_LANE_BLOCK]


def _ssm(su, nb, bw, cw_mat, a_re, a_im, d, w_glu, s0_re, s0_im):
    nt, nl, _, tt, lw = su.shape
    C = nl * lw
    rows = tt * nb
    cw = min(SSM_BLOCK_STATES, 128 * max(1, 64 // nb))
    tile = pl.BlockSpec((None, nl, rows, lw), lambda i: (i, 0, 0, 0))
    y, sre, sim = pl.pallas_call(
        functools.partial(_ssm_body, nb=nb, tt=tt, cw=cw),
        grid=(nt,),
        in_specs=[tile, _const_spec(bw.shape), _const_spec(cw_mat.shape),
                  _const_spec((1, SSM_STATES)), _const_spec((1, SSM_STATES)),
                  _const_spec((1, C)), _const_spec(w_glu.shape),
                  _const_spec((nb, SSM_STATES)), _const_spec((nb, SSM_STATES))],
        out_specs=[tile,
                   pl.BlockSpec((nb, SSM_STATES), lambda i: (0, 0)),
                   pl.BlockSpec((nb, SSM_STATES), lambda i: (0, 0))],
        out_shape=[jax.ShapeDtypeStruct((nt, nl, rows, lw), F32),
                   jax.ShapeDtypeStruct((nb, SSM_STATES), F32),
                   jax.ShapeDtypeStruct((nb, SSM_STATES), F32)],
        scratch_shapes=[pltpu.VMEM((rows, C), F32), pltpu.VMEM((rows, 2 * SSM_BLOCK_STATES), F32),
                        pltpu.VMEM((rows, 2 * SSM_BLOCK_STATES), F32), pltpu.VMEM((rows, C), F32)],
        compiler_params=_params("arbitrary"),
        name="ssm",
    )(su.reshape(nt, nl, rows, lw), bw, cw_mat, a_re.reshape(1, -1), a_im.reshape(1, -1), d.reshape(1, C), w_glu,
      s0_re, s0_im)
    return y.reshape(su.shape), sre, sim


def _ssm_weights(lam_re, lam_im, log_dt, b_re, b_im, c_re, c_im):
    dt = jnp.exp(log_dt.astype(F32))[:, None]
    lr, li = lam_re.astype(F32), lam_im.astype(F32)
    mag = jnp.exp(lr * dt)
    ab_re, ab_im = mag * jnp.cos(li * dt), mag * jnp.sin(li * dt)
    nr, ni = ab_re - 1.0, ab_im
    den = lr * lr + li * li
    f_re = (nr * lr + ni * li) / den
    f_im = (ni * lr - nr * li) / den
    br, bi = b_re.astype(F32), b_im.astype(F32)
    bb_re = f_re[..., None] * br - f_im[..., None] * bi
    bb_im = f_re[..., None] * bi + f_im[..., None] * br
    gpb = SSM_GROUPS // SSM_BLOCKS
    eye = jnp.eye(gpb, dtype=F32)

    def in_block(bb):
        bb = bb.reshape(SSM_BLOCKS, gpb, SSM_STATE, SSM_GROUP)
        m = jnp.einsum('jgpc,gh->jgchp', bb, eye)
        return m.reshape(SSM_BLOCKS, gpb * SSM_GROUP, gpb * SSM_STATE)

    def out_block(cc):
        cc = cc.reshape(SSM_BLOCKS, gpb, SSM_GROUP, SSM_STATE)
        m = jnp.einsum('jgcp,gh->jgphc', cc, eye)
        return m.reshape(SSM_BLOCKS, gpb * SSM_STATE, gpb * SSM_GROUP)

    bw = jnp.concatenate([in_block(bb_re), in_block(bb_im)], axis=-1).astype(BF16)
    cwm = jnp.concatenate([out_block(c_re.astype(F32)), -out_block(c_im.astype(F32))], axis=1).astype(BF16)
    return bw, cwm, ab_re.reshape(-1), ab_im.reshape(-1)


def _split_q(q):
    qf = q.astype(F32) * (ATTN_HEAD_DIM ** -0.5)
    lane = lax.broadcasted_iota(jnp.int32, (1, ATTN_V_DIM), 1)
    first = lane < ATTN_HEAD_DIM
    return jnp.where(first, qf, 0.0).astype(BF16), jnp.where(first, 0.0, qf).astype(BF16)


def _scores(q, k):
    return lax.dot_general(q, k, (((1,), (1,)), ((), ())), preferred_element_type=F32)


def _online(t, v, m, l, acc):
    m_new = jnp.maximum(m, jnp.max(t, axis=-1, keepdims=True))
    alpha = jnp.exp(m - m_new)
    p = jnp.exp(t - m_new)
    l_new = alpha * l + jnp.sum(p, axis=-1, keepdims=True)
    acc_new = alpha * acc + jnp.dot(p.astype(BF16), v, preferred_element_type=F32)
    return m_new, l_new, acc_new


def _attn_finish(o1, o2, lq1, lk1, lq2, lk2, subg, az, lambda_init):
    lam = (jnp.exp(jnp.sum(lq1 * lk1, axis=-1, keepdims=True))
           - jnp.exp(jnp.sum(lq2 * lk2, axis=-1, keepdims=True)) + lambda_init)
    o = o1 - lam * o2
    o = _rms_rows(o, subg) * (1.0 - lambda_init)
    return o * jax.nn.silu(az.astype(F32))


ATTN_ONES_ROWS = 16
ATTN_HEAD_GROUP = 4
ATTN_STEPS_PER_LOOP = 4


def _bf16_parts(x, n):
    parts = []
    for _ in range(n):
        bits = struct.unpack('<I', struct.pack('<f', x))[0]
        bits = (bits + 0x7FFF + ((bits >> 16) & 1)) & 0xFFFF0000
        part = struct.unpack('<f', struct.pack('<I', bits))[0]
        parts.append(part)
        x -= part
    return parts


LOG2E = math.log2(math.e)
LOG2E_PARTS = _bf16_parts(LOG2E, 3)
ATTN_Q_SCALE = ATTN_HEAD_DIM ** -0.5 * LOG2E


def _attn_prompt_body(slope_ref, lq1_ref, lk1_ref, lq2_ref, lk2_ref, subg_ref,
                      q_ref, k_ref, v_ref, az_ref, out_ref, kx, vxt, corr, wq, *state,
                      tq, nk, heads, lambda_init):
    qi = pl.program_id(2)
    hd = ATTN_V_DIM
    rows_acc = hd + ATTN_ONES_ROWS
    lane = lax.broadcasted_iota(jnp.int32, (tq, hd), 1)
    row = lax.broadcasted_iota(jnp.int32, (tq, hd), 0).astype(F32)
    slopes = [slope_ref[g, 0:1, 0:1] for g in range(heads)]
    cols = [slice(g * hd, (g + 1) * hd) for g in range(heads)]
    nc = 2 * heads
    mst, tmax_a, tmax_b, ast, t_a, t_b = (state[n * nc:(n + 1) * nc] for n in range(6))
    tmax_s, t_s = (tmax_a, tmax_b), (t_a, t_b)

    @pl.when(qi == 0)
    def _():
        ones = jnp.ones((ATTN_ONES_ROWS, tq), BF16)
        c = lax.broadcasted_iota(jnp.int32, (tq, tq), 0)
        r = lax.broadcasted_iota(jnp.int32, (tq, tq), 1)
        visible = (c >> 6) <= (r >> 6)
        ahead = jnp.where(c > r, (c - r).astype(F32), 0.0)
        for g in range(heads):
            kfeat = jnp.where(lane < 3, row, 0.0)
            for n, part in enumerate(LOG2E_PARTS):
                kfeat = jnp.where(lane == 3 + n, -slopes[g] * part, kfeat)
            kfeat = kfeat.astype(BF16)
            for j in range(nk):
                rows = slice(j * tq, (j + 1) * tq)
                kx[g, rows, 0:hd] = k_ref[rows, cols[g]]
                kx[g, rows, hd:2 * hd] = kfeat
                vxt[g, j, 0:hd, :] = v_ref[rows, cols[g]].astype(F32).T.astype(BF16)
                vxt[g, j, hd:rows_acc, :] = ones
            corr[g] = jnp.where(visible, (-2.0 * LOG2E) * slopes[g] * ahead, NEG_INF)

    first = lane < ATTN_HEAD_DIM
    for g in range(heads):
        qf = q_ref[:, cols[g]].astype(F32)
        qfeat = jnp.where((lane >= 3) & (lane < 6), row, 0.0)
        for n, part in enumerate(LOG2E_PARTS):
            qfeat = jnp.where(lane == n, slopes[g] * part, qfeat)
        wq[2 * g] = jnp.concatenate([jnp.where(first, qf, 0.0), qfeat], axis=1).T.astype(BF16)
        wq[2 * g + 1] = jnp.concatenate([jnp.where(first, 0.0, qf), qfeat], axis=1).T.astype(BF16)
    chains = [(g, 2 * g + n) for g in range(heads) for n in range(2)]
    for _, c in chains:
        mst[c][...] = jnp.full((1, tq), NEG_INF, F32)
        ast[c][...] = jnp.zeros((rows_acc, tq), F32)

    def scores(j, buf):
        kstart = pl.multiple_of(j * tq, tq)
        for g, c in chains:
            t = jnp.dot(kx[g, pl.ds(kstart, tq), :], wq[c], preferred_element_type=F32)
            t_s[buf][c][...] = t
            tmax_s[buf][c][...] = jnp.max(t, axis=0, keepdims=True)

    def update(j, buf, diagonal=False):
        for g, c in chains:
            if diagonal:
                t = t_s[buf][c][...] + corr[g]
                tmax = jnp.max(t, axis=0, keepdims=True)
                shift = jnp.zeros((1, 1), F32)
            else:
                t, tmax = t_s[buf][c][...], tmax_s[buf][c][...]
                shift = (-LOG2E) * slopes[g] * ((qi - j) * tq).astype(F32)
            m_old = mst[c][...]
            m_new = jnp.maximum(m_old, tmax + shift)
            mst[c][...] = m_new
            p = jnp.exp2(t + (shift - m_new)).astype(BF16)
            ast[c][...] = jnp.exp2(m_old - m_new) * ast[c][...] + jnp.dot(vxt[g, j], p,
                                                                         preferred_element_type=F32)

    def step(j, buf):
        scores(j + 1, 1 - buf)
        update(j, buf)

    def group(k, carry):
        for s in range(ATTN_STEPS_PER_LOOP):
            step(ATTN_STEPS_PER_LOOP * k + s, s % 2)
        return carry

    scores(0, 0)
    groups = qi // ATTN_STEPS_PER_LOOP
    lax.fori_loop(0, groups, group, 0)
    for rest in range(ATTN_STEPS_PER_LOOP):
        @pl.when(qi - groups * ATTN_STEPS_PER_LOOP == rest)
        def _(rest=rest):
            for s in range(rest):
                step(groups * ATTN_STEPS_PER_LOOP + s, s % 2)
            update(qi, rest % 2, diagonal=True)

    lam = (jnp.exp(jnp.sum(lq1_ref[...] * lk1_ref[...], axis=-1, keepdims=True))
           - jnp.exp(jnp.sum(lq2_ref[...] * lk2_ref[...], axis=-1, keepdims=True)) + lambda_init)
    for g in range(heads):
        a1, a2 = ast[2 * g][...], ast[2 * g + 1][...]
        o = (a1[0:hd, :] / a1[hd:hd + 1, :] - lam * (a2[0:hd, :] / a2[hd:hd + 1, :])).T
        o = _rms_rows(o, subg_ref[...]) * (1.0 - lambda_init)
        out_ref[:, cols[g]] = (o * jax.nn.silu(az_ref[:, cols[g]].astype(F32))).astype(out_ref.dtype)


def _attn_small_inputs(slopes, lq1, lk1, lq2, lk2, subg):
    slope_rows = jnp.broadcast_to(slopes[:, None, None], (ATTN_HEADS, 1, 128))
    r = lambda a: a.astype(F32).reshape(1, -1)
    return slope_rows, r(lq1), r(lk1), r(lq2), r(lk2), r(subg)


def _small_specs():
    return [pl.BlockSpec((None, 1, 128), lambda b, h, *_: (h, 0, 0)),
            _const_spec((1, ATTN_HEAD_DIM)), _const_spec((1, ATTN_HEAD_DIM)),
            _const_spec((1, ATTN_HEAD_DIM)), _const_spec((1, ATTN_HEAD_DIM)),
            _const_spec((1, ATTN_V_DIM))]


def _attn_prompt(q, k, v, az, small, lambda_init):
    B, T, W = q.shape
    tq = _row_tile(T, 256)
    assert tq % CHUNK == 0 and tq <= 256
    nk = T // tq
    hd = ATTN_V_DIM
    hg = ATTN_HEAD_GROUP
    rows_acc = hd + ATTN_ONES_ROWS
    tile = pl.BlockSpec((None, tq, hg * hd), lambda b, h, i: (b, i, h))
    full = pl.BlockSpec((None, T, hg * hd), lambda b, h, i: (b, 0, h))
    small_specs = [pl.BlockSpec((hg, 1, 128), lambda b, h, i: (h, 0, 0))] + _small_specs()[1:]
    return pl.pallas_call(
        functools.partial(_attn_prompt_body, tq=tq, nk=nk, heads=hg, lambda_init=lambda_init),
        grid=(B, ATTN_HEADS // hg, nk),
        in_specs=small_specs + [tile, full, full, tile],
        out_specs=tile,
        out_shape=jax.ShapeDtypeStruct((B, T, W), BF16),
        scratch_shapes=[pltpu.VMEM((hg, T, 2 * hd), BF16),
                        pltpu.VMEM((hg, nk, rows_acc, tq), BF16),
                        pltpu.VMEM((hg, tq, tq), F32),
                        pltpu.VMEM((2 * hg, 2 * hd, tq), BF16)]
        + [pltpu.VMEM((1, tq), F32)] * (3 * 2 * hg) + [pltpu.VMEM((rows_acc, tq), F32)] * (2 * hg)
        + [pltpu.VMEM((tq, tq), F32)] * (2 * 2 * hg),
        compiler_params=_params("parallel", "parallel", "arbitrary"),
        name="attn_prompt",
    )(*small, q, k, v, az)


def _attn_sample_body(slope_ref, lq1_ref, lk1_ref, lq2_ref, lk2_ref, subg_ref,
                      q_ref, kp_ref, vp_ref, kn_ref, vn_ref, az_ref, out_ref, m_s, l_s, acc_s,
                      *, past, pc, lambda_init):
    j = pl.program_id(1)
    T = q_ref.shape[0]
    hd = ATTN_V_DIM
    heads = ATTN_HEADS
    cols = [slice(h * hd, (h + 1) * hd) for h in range(heads)]
    slopes = [slope_ref[h, 0:1, 0:1] for h in range(heads)]
    qpos = past + (lax.broadcasted_iota(jnp.int32, (2 * T, 1), 0) & (T - 1))
    qposf = qpos.astype(F32)

    @pl.when(j == 0)
    def _():
        m_s[...] = jnp.full(m_s.shape, NEG_INF, F32)
        l_s[...] = jnp.zeros(l_s.shape, F32)
        acc_s[...] = jnp.zeros(acc_s.shape, F32)

    qs = [jnp.concatenate(_split_q(q_ref[:, cols[h]]), axis=0) for h in range(heads)]

    def attend(keys, vals, bias_of):
        ss = [_scores(qs[h], keys[h]) + bias_of(h) for h in range(heads)]
        for h in range(heads):
            m_s[h], l_s[h], acc_s[h] = _online(ss[h], vals[h], m_s[h], l_s[h], acc_s[h])

    kposf = (j * pc + lax.broadcasted_iota(jnp.int32, (1, pc), 1)).astype(F32)
    attend([kp_ref[pl.ds(h, pc, stride=heads), :].astype(BF16) for h in range(heads)],
           [vp_ref[pl.ds(h, pc, stride=heads), :].astype(BF16) for h in range(heads)],
           lambda h: slopes[h] * kposf - slopes[h] * qposf)

    @pl.when(j == pl.num_programs(1) - 1)
    def _():
        kpos = past + lax.broadcasted_iota(jnp.int32, (1, T), 1)
        dist = jnp.abs(qpos - kpos).astype(F32)
        visible = (kpos >> 6) <= (qpos >> 6)
        attend([kn_ref[:, cols[h]].astype(BF16) for h in range(heads)],
               [vn_ref[:, cols[h]].astype(BF16) for h in range(heads)],
               lambda h: jnp.where(visible, -slopes[h] * dist, NEG_INF))
        for h in range(heads):
            o = acc_s[h] / l_s[h]
            o = _attn_finish(o[0:T], o[T:2 * T], lq1_ref[...], lk1_ref[...], lq2_ref[...], lk2_ref[...],
                             subg_ref[...], az_ref[:, cols[h]], lambda_init)
            out_ref[:, cols[h]] = o.astype(out_ref.dtype)


def _attn_sample(q, k_cache, v_cache, k_stack, v_stack, layer, az, small, lambda_init):
    B, T, W = q.shape
    depth, _, P = k_cache.shape[:3]
    assert P % CHUNK == 0 and T == CHUNK and P > 0
    hd = ATTN_V_DIM
    pc = _row_tile(P, 1024)
    tile = pl.BlockSpec((None, T, W), lambda b, j: (b, 0, 0))
    new = pl.BlockSpec((None, None, T, W), lambda b, j: (layer, b, 0, 0))
    pastb = pl.BlockSpec((None, None, pc * ATTN_HEADS, hd), lambda b, j: (layer, b, j, 0))
    small_specs = [pl.BlockSpec((ATTN_HEADS, 1, 128), lambda b, j: (0, 0, 0))] + _small_specs()[1:]
    rows = lambda a: a.reshape(depth, B, P * ATTN_HEADS, hd)
    return pl.pallas_call(
        functools.partial(_attn_sample_body, past=P, pc=pc, lambda_init=lambda_init),
        grid=(B, P // pc),
        in_specs=small_specs + [tile, pastb, pastb, new, new, tile],
        out_specs=tile,
        out_shape=jax.ShapeDtypeStruct((B, T, W), BF16),
        scratch_shapes=[pltpu.VMEM((ATTN_HEADS, 2 * T, 1), F32), pltpu.VMEM((ATTN_HEADS, 2 * T, 1), F32),
                        pltpu.VMEM((ATTN_HEADS, 2 * T, hd), F32)],
        compiler_params=_params("parallel", "arbitrary"),
        name="attn_sample",
    )(*small, q, rows(k_cache), rows(v_cache), k_stack, v_stack, az)


def _memat_body(q_ref, k_ref, v_ref, out_ref):
    q = q_ref[...]
    k = k_ref[...]
    v = v_ref[...]
    lane = lax.broadcasted_iota(jnp.int32, (1, MEM_WIDTH), 1)
    mine = [(lane >= h * MEM_HEAD_DIM) & (lane < (h + 1) * MEM_HEAD_DIM) for h in range(MEM_HEADS)]
    ss = [_scores(q, jnp.where(mine[h], k, 0.0).astype(BF16)) * (MEM_HEAD_DIM ** -0.5) for h in range(MEM_HEADS)]
    acc = jnp.zeros(out_ref.shape, F32)
    for h in range(MEM_HEADS):
        vh = jnp.where(mine[h], v, 0.0).astype(BF16)
        p = jnp.exp(ss[h] - jnp.max(ss[h], axis=-1, keepdims=True))
        a = p / jnp.sum(p, axis=-1, keepdims=True)
        acc = acc + jnp.dot(a.astype(BF16), vh, preferred_element_type=F32)
    out_ref[...] = acc.astype(out_ref.dtype)


def _memat(mq, mk, mv):
    B, T, W = mq.shape
    M = mk.shape[1]
    tm = _row_tile(T, 512)
    return pl.pallas_call(
        _memat_body,
        grid=(B, T // tm),
        in_specs=[pl.BlockSpec((None, tm, W), lambda b, i: (b, i, 0)),
                  pl.BlockSpec((None, M, W), lambda b, i: (b, 0, 0)),
                  pl.BlockSpec((None, M, W), lambda b, i: (b, 0, 0))],
        out_specs=pl.BlockSpec((None, tm, W), lambda b, i: (b, i, 0)),
        out_shape=jax.ShapeDtypeStruct((B, T, W), BF16),
        compiler_params=_params("parallel", "parallel"),
        name="memat",
    )(mq, mk, mv)


def _merge_body(x_ref, g_ref, pool_ref, ssm_ref, sz_ref, attn_ref, mem_ref,
                wp_ref, ws_ref, wa_ref, wm_ref, wg_ref, wo_ref, fg_ref, out_ref, *, final_norm):
    x = x_ref[...]
    D = x.shape[-1]
    h = _rms_rows(x, g_ref[...]).astype(BF16)
    rows = sz_ref.shape[0]
    ssm_y = jnp.concatenate([ssm_ref[:, j].reshape(rows, -1) for j in range(ssm_ref.shape[1])], axis=1)
    ssm_g = (ssm_y * jax.nn.silu(sz_ref[...].astype(F32))).astype(BF16)
    merged = jnp.zeros(x.shape, F32)
    branches = ((pool_ref[...], wp_ref), (ssm_g, ws_ref), (attn_ref[...], wa_ref), (mem_ref[...], wm_ref))
    for n, (a, w_ref) in enumerate(branches):
        gate = jax.nn.sigmoid(jnp.dot(h, wg_ref[:, n * D:(n + 1) * D], preferred_element_type=F32))
        merged = merged + gate * jnp.dot(a, w_ref[...], preferred_element_type=F32)
    y = x + jnp.dot(merged.astype(BF16), wo_ref[...], preferred_element_type=F32)
    if final_norm:
        y = _rms_rows(y, fg_ref[...])
    out_ref[...] = y


def _merge(x, g, pool_g, ssm_y, sz, attn_g, mem_o, wp, ws, wa, wm, wg, wo, fg, final_norm):
    B, T, D = x.shape
    tm = _row_tile(T, 512)
    tt = ssm_y.shape[3]
    bt = lambda wd: pl.BlockSpec((None, tm, wd), lambda b, i: (b, i, 0))
    return pl.pallas_call(
        functools.partial(_merge_body, final_norm=final_norm),
        grid=(B, T // tm),
        in_specs=[bt(D), _const_spec((1, D)), bt(POOL_WIDTH),
                  pl.BlockSpec((tm // tt, SSM_BLOCKS, None, tt, SSM_LANE_BLOCK), lambda b, i: (i, 0, b, 0, 0)),
                  bt(SSM_WIDTH),
                  bt(ATTN_WIDTH), bt(MEM_WIDTH),
                  _const_spec(wp.shape), _const_spec(ws.shape), _const_spec(wa.shape), _const_spec(wm.shape),
                  _const_spec(wg.shape), _const_spec(wo.shape), _const_spec((1, D))],
        out_specs=bt(D),
        out_shape=jax.ShapeDtypeStruct((B, T, D), F32),
        compiler_params=_params("parallel", "parallel"),
        name="merge",
    )(x, g.reshape(1, D), pool_g, ssm_y, sz, attn_g, mem_o, wp, ws, wa, wm, wg, wo, fg.reshape(1, D))


def _trunk_layer(x, layer, depth, pool_hist, ssm_re, ssm_im, caches, kv_stacks, mem_k, mem_v, lw, lambda_init,
                 final_g):
    B, T, D = x.shape
    pu, pz, su, sz, q, k_stack, k16, v_stack, v16, az, mq = _trunk_proj(
        x, lw['norm_g'], lw['w_in_main'], layer, depth, kv_stacks, scaled_q=caches is None)
    pos0 = 0 if caches is None else caches[0].shape[2]
    hist16 = jnp.concatenate([jnp.zeros((B, 1, POOL_WIDTH), F32), pool_hist], axis=1)
    pool_g, hist_new = _pool(pu, hist16, pz, lw['pool_w_bd'], lw['pool_scale'], pos0)
    ssm_y, sre, sim = _ssm(su, B, lw['ssm_bw'], lw['ssm_cw'], lw['ssm_a_re'], lw['ssm_a_im'], lw['ssm_d'],
                           lw['w_glu'], ssm_re.reshape(B, SSM_STATES), ssm_im.reshape(B, SSM_STATES))
    if caches is None:
        attn_g = _attn_prompt(q, k16, v16, az, lw['attn_small'], lambda_init)
    else:
        attn_g = _attn_sample(q, caches[0], caches[1], k_stack, v_stack, layer, az, lw['attn_small'], lambda_init)
    mem_o = _memat(mq, mem_k, mem_v)
    y = _merge(x, lw['norm_g'], pool_g, ssm_y, sz, attn_g, mem_o,
               lw['w_br_pool'], lw['w_br_ssm'], lw['w_br_attn'], lw['w_br_mem'], lw['w_in_gates'], lw['w_out'],
               final_g, layer == depth - 1)
    return (y, hist_new[:, 1:], sre.reshape(B, SSM_GROUPS, SSM_STATE), sim.reshape(B, SSM_GROUPS, SSM_STATE),
            (k_stack, v_stack))


def kernel(x_prompt, x_sample, mem_prompt, cache_attn_k, cache_attn_v, cache_mem_k, cache_mem_v, state_pool, state_ssm_re, state_ssm_im, norm_g, w_in, pool_w, pool_scale, ssm_lambda_re, ssm_lambda_im, ssm_log_dt, ssm_b_re, ssm_b_im, ssm_c_re, ssm_c_im, ssm_d, ssm_w_glu, attn_lq1, attn_lk1, attn_lq2, attn_lk2, attn_subln_g, mem_norm_g, w_mem_kv, w_br_pool, w_br_ssm, w_br_attn, w_br_mem, w_out, final_norm_g):
    depth = w_in.shape[0]
    B = x_prompt.shape[0]
    Bd = x_sample.shape[0]
    n_main = w_in.shape[-1] - N_BRANCH * x_prompt.shape[-1]
    slopes = jnp.asarray([2.0 ** (-8.0 * (h + 1) / ATTN_HEADS) for h in range(ATTN_HEADS)], F32)
    xp, xs = x_prompt, x_sample
    outs = {n: [] for n in ('mkp', 'mvp', 'poolp', 'srep', 'simp', 'pools', 'sres', 'sims')}
    kv_p = kv_s = None
    for l in range(depth):
        bw, cwm, a_re, a_im = _ssm_weights(ssm_lambda_re[l], ssm_lambda_im[l], ssm_log_dt[l],
                                           ssm_b_re[l], ssm_b_im[l], ssm_c_re[l], ssm_c_im[l])
        eye = jnp.eye(len(POOL_WINDOWS), dtype=F32)
        pool_w_bd = jnp.einsum('gcd,gh->gchd', pool_w[l].astype(F32), eye).reshape(POOL_WIDTH, POOL_WIDTH)
        lw = {
            'norm_g': norm_g[l], 'w_in_main': w_in[l, :, :n_main].astype(BF16),
            'w_in_gates': w_in[l, :, n_main:].astype(BF16),
            'pool_w_bd': pool_w_bd.astype(BF16), 'pool_scale': pool_scale[l],
            'ssm_bw': bw, 'ssm_cw': cwm, 'ssm_a_re': a_re, 'ssm_a_im': a_im, 'ssm_d': ssm_d[l],
            'w_glu': ssm_w_glu[l].astype(BF16),
            'attn_small': _attn_small_inputs(slopes, attn_lq1[l], attn_lk1[l], attn_lq2[l], attn_lk2[l],
                                             attn_subln_g[l]),
            'w_br_pool': w_br_pool[l].astype(BF16), 'w_br_ssm': w_br_ssm[l].astype(BF16),
            'w_br_attn': w_br_attn[l].astype(BF16), 'w_br_mem': w_br_mem[l].astype(BF16),
            'w_out': w_out[l].astype(BF16),
        }
        lambda_init = 0.8 - 0.6 * math.exp(-0.3 * l)
        mk_p, mv_p = _memory_kv(mem_prompt, mem_norm_g[l], w_mem_kv[l].astype(BF16))
        zero_pool = jnp.zeros((B, POOL_HIST, POOL_WIDTH), F32)
        zero_ssm = jnp.zeros((B, SSM_GROUPS, SSM_STATE), F32)
        xp, pool_p, sre_p, sim_p, kv_p = _trunk_layer(
            xp, l, depth, zero_pool, zero_ssm, zero_ssm, None, kv_p, mk_p, mv_p, lw, lambda_init, final_norm_g)
        M = cache_mem_k.shape[2]
        xs, pool_s, sre_s, sim_s, kv_s = _trunk_layer(
            xs, l, depth, state_pool[l], state_ssm_re[l], state_ssm_im[l], (cache_attn_k, cache_attn_v), kv_s,
            cache_mem_k[l].reshape(Bd, M, MEM_WIDTH), cache_mem_v[l].reshape(Bd, M, MEM_WIDTH),
            lw, lambda_init, final_norm_g)
        mshape = (B, mem_prompt.shape[1], MEM_HEADS, MEM_HEAD_DIM)
        for n, a in (('mkp', mk_p.reshape(mshape)), ('mvp', mv_p.reshape(mshape)),
                     ('poolp', pool_p), ('srep', sre_p), ('simp', sim_p),
                     ('pools', pool_s), ('sres', sre_s), ('sims', sim_s)):
            outs[n].append(a)
    st = {n: jnp.stack(a) for n, a in outs.items()}
    heads = lambda a: a.reshape(a.shape[:3] + (ATTN_HEADS, ATTN_V_DIM))
    return (xp, xs, heads(kv_p[0]), heads(kv_p[1]), st['mkp'], st['mvp'], st['poolp'], st['srep'], st['simp'],
            heads(kv_s[0]), heads(kv_s[1]), st['pools'], st['sres'], st['sims'])
```

```python
import functools
import math
import struct

import jax
import jax.numpy as jnp
from jax import lax
from jax.experimental import pallas as pl
from jax.experimental.pallas import tpu as pltpu

F32 = jnp.float32
BF16 = jnp.bfloat16

CHUNK = 64
EPS = 1e-6
NEG_INF = -1e30

POOL_WIDTH = 768
POOL_WINDOWS = (2, 4, 8, 16)
POOL_GROUP = POOL_WIDTH // len(POOL_WINDOWS)
POOL_HIST = 15
POOL_PAD = 32

SSM_WIDTH = 768
SSM_GROUP = 16
SSM_GROUPS = SSM_WIDTH // SSM_GROUP
SSM_STATE = 64
SSM_STATES = SSM_GROUPS * SSM_STATE
SSM_LANE_BLOCK = 128
SSM_BLOCKS = SSM_WIDTH // SSM_LANE_BLOCK
SSM_BLOCK_STATES = SSM_STATES // SSM_BLOCKS

ATTN_HEADS = 8
ATTN_HEAD_DIM = 64
ATTN_V_DIM = 128
ATTN_WIDTH = ATTN_HEADS * ATTN_V_DIM

MEM_HEADS = 4
MEM_HEAD_DIM = 64
MEM_WIDTH = MEM_HEADS * MEM_HEAD_DIM

N_BRANCH = 4

VMEM_LIMIT_BYTES = 56 * 1024 * 1024


def _params(*sem):
    return pltpu.CompilerParams(dimension_semantics=sem, vmem_limit_bytes=VMEM_LIMIT_BYTES)


def _const_spec(shape):
    nd = len(shape)
    return pl.BlockSpec(shape, lambda *_: (0,) * nd, pipeline_mode=pl.Buffered(1))


def _rms_rows(x, g):
    return x * lax.rsqrt(jnp.mean(x * x, axis=-1, keepdims=True) + EPS) * g


def _proj_body(x_ref, g_ref, w_ref, *out_refs, segs):
    h = _rms_rows(x_ref[...], g_ref[...]).astype(BF16)
    outs = iter(out_refs)
    for a, b, modes in segs:
        y = jnp.dot(h, w_ref[:, a:b], preferred_element_type=F32)
        for mode in modes:
            o_ref = next(outs)
            if mode == 'slots':
                for s in range(o_ref.shape[0]):
                    o_ref[s] = y.astype(o_ref.dtype)
            elif mode == 'tiles':
                nt, nl, tt, lw = o_ref.shape
                for j in range(nl):
                    o_ref[:, j] = y[:, j * lw:(j + 1) * lw].astype(o_ref.dtype).reshape(nt, tt, lw)
            elif mode == 'scaled_q':
                o_ref[...] = (y * ATTN_Q_SCALE).astype(o_ref.dtype)
            else:
                o_ref[...] = y.astype(o_ref.dtype)


def _row_tile(t, want):
    tm = min(t, want)
    assert t % tm == 0
    return tm


def _proj(x, g, w, segs, out_shapes, out_specs, tm, carried=None):
    B, T, D = x.shape
    carried = carried or {}

    def body(x_ref, g_ref, w_ref, *refs):
        _proj_body(x_ref, g_ref, w_ref, *refs[len(carried):], segs=segs)

    return pl.pallas_call(
        body,
        grid=(B, T // tm),
        in_specs=[pl.BlockSpec((None, tm, D), lambda b, i: (b, i, 0)),
                  _const_spec((1, D)), _const_spec(w.shape)]
        + [pl.BlockSpec(memory_space=pl.ANY)] * len(carried),
        out_specs=out_specs,
        out_shape=out_shapes,
        input_output_aliases={3 + n: o for n, o in enumerate(carried)},
        compiler_params=_params("parallel", "parallel"),
        name="proj",
    )(x, g.reshape(1, D), w, *carried.values())


def _ssm_time_tile(T, nb):
    return _row_tile(T, max(8, 1024 // nb))


def _trunk_proj(x, g, w, layer, depth, kv_stacks, scaled_q):
    B, T, D = x.shape
    tm = _row_tile(T, 512)
    tt = _ssm_time_tile(T, B)
    bounds = [0, 768, 1536, 2304, 3072, 4096, 5120, 6144, 7168, 7424]
    first = kv_stacks is None
    stack_mode = 'slots' if first else 'rows'
    q_mode = 'scaled_q' if scaled_q else 'rows'
    modes = [('rows',), ('rows',), ('tiles',), ('rows',), (q_mode,), (stack_mode, 'rows'), (stack_mode, 'rows'),
             ('rows',), ('rows',)]
    segs = tuple(zip(bounds[:-1], bounds[1:], modes))
    bt = lambda wd, dt: jax.ShapeDtypeStruct((B, T, wd), dt)
    s_bt = lambda wd: pl.BlockSpec((None, tm, wd), lambda b, i: (b, i, 0))
    tiles = jax.ShapeDtypeStruct((T // tt, SSM_BLOCKS, B, tt, SSM_LANE_BLOCK), F32)
    s_tiles = pl.BlockSpec((tm // tt, SSM_BLOCKS, None, tt, SSM_LANE_BLOCK), lambda b, i: (i, 0, b, 0, 0))
    stack = jax.ShapeDtypeStruct((depth, B, T, ATTN_WIDTH), F32)
    if first:
        s_stack = pl.BlockSpec((depth, None, tm, ATTN_WIDTH), lambda b, i: (0, b, i, 0))
    else:
        s_stack = pl.BlockSpec((None, None, tm, ATTN_WIDTH), lambda b, i: (layer, b, i, 0))
    shapes = [bt(768, F32), bt(768, BF16), tiles, bt(768, BF16), bt(1024, BF16),
              stack, bt(1024, BF16), stack, bt(1024, BF16), bt(1024, BF16), bt(256, BF16)]
    specs = [s_bt(768), s_bt(768), s_tiles, s_bt(768), s_bt(1024),
             s_stack, s_bt(1024), s_stack, s_bt(1024), s_bt(1024), s_bt(256)]
    carried = None if first else {5: kv_stacks[0], 7: kv_stacks[1]}
    return _proj(x, g, w, segs, shapes, specs, tm, carried)


def _memory_kv(mem, g, w):
    B, M, D = mem.shape
    segs = ((0, MEM_WIDTH, ('rows',)), (MEM_WIDTH, 2 * MEM_WIDTH, ('rows',)))
    shapes = [jax.ShapeDtypeStruct((B, M, MEM_WIDTH), F32)] * 2
    specs = [pl.BlockSpec((None, M, MEM_WIDTH), lambda b, i: (b, i, 0))] * 2
    return _proj(mem, g, w, segs, shapes, specs, M)


def _pool_body(u_ref, hist_ref, z_ref, w_ref, scale_ref, out_ref, hist_out_ref,
               ext, b2, b4, b8, *, tt, pos0):
    i = pl.program_id(1)
    n = POOL_PAD + tt

    @pl.when(i == 0)
    def _():
        ext[0:16, :] = jnp.zeros((16, POOL_WIDTH), F32)
        ext[16:32, :] = hist_ref[...]

    @pl.when(i > 0)
    def _():
        ext[16:32, :] = ext[tt + 16:tt + 32, :]

    u = u_ref[...]
    ext[32:n, :] = u
    b2[8:n, :] = ext[8:n, :] + ext[7:n - 1, :]
    b4[16:n, :] = b2[16:n, :] + b2[14:n - 2, :]
    b8[24:n, :] = b4[24:n, :] + b4[20:n - 4, :]
    s2 = b2[32:n, :]
    s4 = b4[32:n, :]
    s8 = b8[32:n, :]
    s16 = s8 + b8[24:n - 8, :]
    col = lax.broadcasted_iota(jnp.int32, (1, POOL_WIDTH), 1)
    wsum = jnp.where(col < POOL_GROUP, s2,
                     jnp.where(col < 2 * POOL_GROUP, s4, jnp.where(col < 3 * POOL_GROUP, s8, s16)))
    win = jnp.where(col < POOL_GROUP, 2, jnp.where(col < 2 * POOL_GROUP, 4, jnp.where(col < 3 * POOL_GROUP, 8, 16)))
    pos = pos0 + i * tt + lax.broadcasted_iota(jnp.int32, (tt, 1), 0)
    cnt = jnp.minimum(win, pos + 1).astype(F32)
    diff = (wsum / cnt - u).astype(BF16)
    y = jnp.dot(diff, w_ref[...], preferred_element_type=F32) * scale_ref[...]
    out_ref[...] = (y * jax.nn.silu(z_ref[...].astype(F32))).astype(out_ref.dtype)
    hist_out_ref[...] = ext[tt + 16:tt + 32, :]


def _pool(pu, hist16, pz, w_bd, scale, pos0):
    B, T, C = pu.shape
    tt = _row_tile(T, 512)
    n = POOL_PAD + tt
    return pl.pallas_call(
        functools.partial(_pool_body, tt=tt, pos0=pos0),
        grid=(B, T // tt),
        in_specs=[pl.BlockSpec((None, tt, C), lambda b, i: (b, i, 0)),
                  pl.BlockSpec((None, 16, C), lambda b, i: (b, 0, 0)),
                  pl.BlockSpec((None, tt, C), lambda b, i: (b, i, 0)),
                  _const_spec((C, C)), _const_spec((1, C))],
        out_specs=[pl.BlockSpec((None, tt, C), lambda b, i: (b, i, 0)),
                   pl.BlockSpec((None, 16, C), lambda b, i: (b, 0, 0))],
        out_shape=[jax.ShapeDtypeStruct((B, T, C), BF16), jax.ShapeDtypeStruct((B, 16, C), F32)],
        scratch_shapes=[pltpu.VMEM((n, C), F32)] * 4,
        compiler_params=_params("parallel", "arbitrary"),
        name="pool",
    )(pu, hist16, pz, w_bd, scale.reshape(1, C))


def _ssm_body(u_ref, bw_ref, cw_ref, are_ref, aim_ref, d_ref, wglu_ref, s0re_ref, s0im_ref,
              out_ref, sre_ref, sim_ref, u, bu_a, bu_b, y, *, nb, tt, cw):
    i = pl.program_id(0)

    @pl.when(i == 0)
    def _():
        sre_ref[...] = s0re_ref[...]
        sim_ref[...] = s0im_ref[...]

    for j in range(SSM_BLOCKS):
        for t in range(tt):
            u[t * nb:(t + 1) * nb, j * SSM_LANE_BLOCK:(j + 1) * SSM_LANE_BLOCK] = u_ref[j, pl.ds(t, nb, stride=tt), :]

    half = SSM_BLOCK_STATES
    for j in range(SSM_BLOCKS):
        lanes = slice(j * SSM_LANE_BLOCK, (j + 1) * SSM_LANE_BLOCK)
        bu = (bu_a, bu_b)[j % 2]
        bu[...] = jnp.dot(u[:, lanes].astype(BF16), bw_ref[j], preferred_element_type=F32)
        for c in range(half // cw):
            st = slice(j * half + c * cw, j * half + (c + 1) * cw)
            cre = slice(c * cw, (c + 1) * cw)
            cim = slice(half + c * cw, half + (c + 1) * cw)
            ar = jnp.broadcast_to(are_ref[:, st], (nb, cw))
            ai = jnp.broadcast_to(aim_ref[:, st], (nb, cw))
            hr, hi = sre_ref[:, st], sim_ref[:, st]
            for t in range(tt):
                rows = slice(t * nb, (t + 1) * nb)
                hr, hi = ar * hr - ai * hi + bu[rows, cre], ar * hi + ai * hr + bu[rows, cim]
                bu[rows, cre] = hr
                bu[rows, cim] = hi
            sre_ref[:, st] = hr
            sim_ref[:, st] = hi
        y[:, lanes] = jnp.dot(bu[...].astype(BF16), cw_ref[j], preferred_element_type=F32)

    yy = y[...] + d_ref[...] * u[...]
    act = jax.nn.gelu(yy).astype(BF16)
    zz = jnp.dot(act, wglu_ref[...], preferred_element_type=F32)
    y[...] = zz[:, :SSM_WIDTH] * jax.nn.sigmoid(zz[:, SSM_WIDTH:])
    for j in range(SSM_BLOCKS):
        for t in range(tt):
            out_ref[j, pl.ds(t, nb, stride=tt), :] = y[t * nb:(t + 1) * nb, j * SSM_LANE_BLOCK:(j + 1) * SSM_LANE_BLOCK]


def _ssm(su, nb, bw, cw_mat, a_re, a_im, d, w_glu, s0_re, s0_im):
    nt, nl, _, tt, lw = su.shape
    C = nl * lw
    rows = tt * nb
    cw = min(SSM_BLOCK_STATES, 128 * max(1, 64 // nb))
    tile = pl.BlockSpec((None, nl, rows, lw), lambda i: (i, 0, 0, 0))
    y, sre, sim = pl.pallas_call(
        functools.partial(_ssm_body, nb=nb, tt=tt, cw=cw),
        grid=(nt,),
        in_specs=[tile, _const_spec(bw.shape), _const_spec(cw_mat.shape),
                  _const_spec((1, SSM_STATES)), _const_spec((1, SSM_STATES)),
                  _const_spec((1, C)), _const_spec(w_glu.shape),
                  _const_spec((nb, SSM_STATES)), _const_spec((nb, SSM_STATES))],
        out_specs=[tile,
                   pl.BlockSpec((nb, SSM_STATES), lambda i: (0, 0)),
                   pl.BlockSpec((nb, SSM_STATES), lambda i: (0, 0))],
        out_shape=[jax.ShapeDtypeStruct((nt, nl, rows, lw), F32),
                   jax.ShapeDtypeStruct((nb, SSM_STATES), F32),
                   jax.ShapeDtypeStruct((nb, SSM_STATES), F32)],
        scratch_shapes=[pltpu.VMEM((rows, C), F32), pltpu.VMEM((rows, 2 * SSM_BLOCK_STATES), F32),
                        pltpu.VMEM((rows, 2 * SSM_BLOCK_STATES), F32), pltpu.VMEM((rows, C), F32)],
        compiler_params=_params("arbitrary"),
        name="ssm",
    )(su.reshape(nt, nl, rows, lw), bw, cw_mat, a_re.reshape(1, -1), a_im.reshape(1, -1), d.reshape(1, C), w_glu,
      s0_re, s0_im)
    return y.reshape(su.shape), sre, sim


def _ssm_weights(lam_re, lam_im, log_dt, b_re, b_im, c_re, c_im):
    dt = jnp.exp(log_dt.astype(F32))[:, None]
    lr, li = lam_re.astype(F32), lam_im.astype(F32)
    mag = jnp.exp(lr * dt)
    ab_re, ab_im = mag * jnp.cos(li * dt), mag * jnp.sin(li * dt)
    nr, ni = ab_re - 1.0, ab_im
    den = lr * lr + li * li
    f_re = (nr * lr + ni * li) / den
    f_im = (ni * lr - nr * li) / den
    br, bi = b_re.astype(F32), b_im.astype(F32)
    bb_re = f_re[..., None] * br - f_im[..., None] * bi
    bb_im = f_re[..., None] * bi + f_im[..., None] * br
    gpb = SSM_GROUPS // SSM_BLOCKS
    eye = jnp.eye(gpb, dtype=F32)

    def in_block(bb):
        bb = bb.reshape(SSM_BLOCKS, gpb, SSM_STATE, SSM_GROUP)
        m = jnp.einsum('jgpc,gh->jgchp', bb, eye)
        return m.reshape(SSM_BLOCKS, gpb * SSM_GROUP, gpb * SSM_STATE)

    def out_block(cc):
        cc = cc.reshape(SSM_BLOCKS, gpb, SSM_GROUP, SSM_STATE)
        m = jnp.einsum('jgcp,gh->jgphc', cc, eye)
        return m.reshape(SSM_BLOCKS, gpb * SSM_STATE, gpb * SSM_GROUP)

    bw = jnp.concatenate([in_block(bb_re), in_block(bb_im)], axis=-1).astype(BF16)
    cwm = jnp.concatenate([out_block(c_re.astype(F32)), -out_block(c_im.astype(F32))], axis=1).astype(BF16)
    return bw, cwm, ab_re.reshape(-1), ab_im.reshape(-1)


def _split_q(q):
    qf = q.astype(F32) * (ATTN_HEAD_DIM ** -0.5)
    lane = lax.broadcasted_iota(jnp.int32, (1, ATTN_V_DIM), 1)
    first = lane < ATTN_HEAD_DIM
    return jnp.where(first, qf, 0.0).astype(BF16), jnp.where(first, 0.0, qf).astype(BF16)


def _scores(q, k):
    return lax.dot_general(q, k, (((1,), (1,)), ((), ())), preferred_element_type=F32)


def _online(t, v, m, l, acc):
    m_new = jnp.maximum(m, jnp.max(t, axis=-1, keepdims=True))
    alpha = jnp.exp(m - m_new)
    p = jnp.exp(t - m_new)
    l_new = alpha * l + jnp.sum(p, axis=-1, keepdims=True)
    acc_new = alpha * acc + jnp.dot(p.astype(BF16), v, preferred_element_type=F32)
    return m_new, l_new, acc_new


def _attn_finish(o1, o2, lq1, lk1, lq2, lk2, subg, az, lambda_init):
    lam = (jnp.exp(jnp.sum(lq1 * lk1, axis=-1, keepdims=True))
           - jnp.exp(jnp.sum(lq2 * lk2, axis=-1, keepdims=True)) + lambda_init)
    o = o1 - lam * o2
    o = _rms_rows(o, subg) * (1.0 - lambda_init)
    return o * jax.nn.silu(az.astype(F32))


ATTN_ONES_ROWS = 16
ATTN_HEAD_GROUP = 4
ATTN_STEPS_PER_LOOP = 4


def _bf16_parts(x, n):
    parts = []
    for _ in range(n):
        bits = struct.unpack('<I', struct.pack('<f', x))[0]
        bits = (bits + 0x7FFF + ((bits >> 16) & 1)) & 0xFFFF0000
        part = struct.unpack('<f', struct.pack('<I', bits))[0]
        parts.append(part)
        x -= part
    return parts


LOG2E = math.log2(math.e)
LOG2E_PARTS = _bf16_parts(LOG2E, 3)
ATTN_Q_SCALE = ATTN_HEAD_DIM ** -0.5 * LOG2E


def _attn_prompt_body(slope_ref, lq1_ref, lk1_ref, lq2_ref, lk2_ref, subg_ref,
                      q_ref, k_ref, v_ref, az_ref, out_ref, kx, vxt, corr, wq, gain_t, *state,
                      tq, nk, heads, lambda_init):
    qi = pl.program_id(2)
    hd = ATTN_V_DIM
    rows_acc = hd + ATTN_ONES_ROWS
    lane = lax.broadcasted_iota(jnp.int32, (tq, hd), 1)
    row = lax.broadcasted_iota(jnp.int32, (tq, hd), 0).astype(F32)
    slopes = [slope_ref[g, 0:1, 0:1] for g in range(heads)]
    cols = [slice(g * hd, (g + 1) * hd) for g in range(heads)]
    nc = 2 * heads
    mst, tmax_a, tmax_b, ast, t_a, t_b = (state[n * nc:(n + 1) * nc] for n in range(6))
    tmax_s, t_s = (tmax_a, tmax_b), (t_a, t_b)

    @pl.when(qi == 0)
    def _():
        ones = jnp.ones((ATTN_ONES_ROWS, tq), BF16)
        gain_t[...] = jnp.broadcast_to(subg_ref[...] * (1.0 - lambda_init), (tq, hd)).T
        c = lax.broadcasted_iota(jnp.int32, (tq, tq), 0)
        r = lax.broadcasted_iota(jnp.int32, (tq, tq), 1)
        visible = (c >> 6) <= (r >> 6)
        ahead = jnp.where(c > r, (c - r).astype(F32), 0.0)
        for g in range(heads):
            kfeat = jnp.where(lane < 3, row, 0.0)
            for n, part in enumerate(LOG2E_PARTS):
                kfeat = jnp.where(lane == 3 + n, -slopes[g] * part, kfeat)
            kfeat = kfeat.astype(BF16)
            for j in range(nk):
                rows = slice(j * tq, (j + 1) * tq)
                kx[g, rows, 0:hd] = k_ref[rows, cols[g]]
                kx[g, rows, hd:2 * hd] = kfeat
                vxt[g, j, 0:hd, :] = v_ref[rows, cols[g]].astype(F32).T.astype(BF16)
                vxt[g, j, hd:rows_acc, :] = ones
            corr[g] = jnp.where(visible, (-2.0 * LOG2E) * slopes[g] * ahead, NEG_INF)
            frow = lax.broadcasted_iota(jnp.int32, (hd, tq), 0)
            qfeat = jnp.where((frow >= 3) & (frow < 6), lax.broadcasted_iota(jnp.int32, (hd, tq), 1).astype(F32), 0.0)
            for n, part in enumerate(LOG2E_PARTS):
                qfeat = jnp.where(frow == n, slopes[g] * part, qfeat)
            zeros = jnp.zeros((ATTN_HEAD_DIM, tq), BF16)
            for n in range(2):
                wq[2 * g + n, hd:2 * hd, :] = qfeat.astype(BF16)
            wq[2 * g, ATTN_HEAD_DIM:hd, :] = zeros
            wq[2 * g + 1, 0:ATTN_HEAD_DIM, :] = zeros

    half = ATTN_HEAD_DIM
    for g in range(heads):
        qt = q_ref[:, cols[g]].astype(F32).T.astype(BF16)
        wq[2 * g, 0:half, :] = qt[0:half]
        wq[2 * g + 1, half:hd, :] = qt[half:hd]
    chains = [(g, 2 * g + n) for g in range(heads) for n in range(2)]
    for _, c in chains:
        mst[c][...] = jnp.full((1, tq), NEG_INF, F32)
        ast[c][...] = jnp.zeros((rows_acc, tq), F32)

    def scores(j, buf):
        kstart = pl.multiple_of(j * tq, tq)
        for g, c in chains:
            t = jnp.dot(kx[g, pl.ds(kstart, tq), :], wq[c], preferred_element_type=F32)
            t_s[buf][c][...] = t
            tmax_s[buf][c][...] = jnp.max(t, axis=0, keepdims=True)

    def update(j, buf, diagonal=False):
        for g, c in chains:
            if diagonal:
                t = t_s[buf][c][...] + corr[g]
                tmax = jnp.max(t, axis=0, keepdims=True)
                shift = jnp.zeros((1, 1), F32)
            else:
                t, tmax = t_s[buf][c][...], tmax_s[buf][c][...]
                shift = (-LOG2E) * slopes[g] * ((qi - j) * tq).astype(F32)
            m_old = mst[c][...]
            m_new = jnp.maximum(m_old, tmax + shift)
            mst[c][...] = m_new
            p = jnp.exp2(t + (shift - m_new)).astype(BF16)
            ast[c][...] = jnp.exp2(m_old - m_new) * ast[c][...] + jnp.dot(vxt[g, j], p,
                                                                         preferred_element_type=F32)

    def step(j, buf):
        scores(j + 1, 1 - buf)
        update(j, buf)

    def group(k, carry):
        for s in range(ATTN_STEPS_PER_LOOP):
            step(ATTN_STEPS_PER_LOOP * k + s, s % 2)
        return carry

    scores(0, 0)
    groups = qi // ATTN_STEPS_PER_LOOP
    lax.fori_loop(0, groups, group, 0)
    for rest in range(ATTN_STEPS_PER_LOOP):
        @pl.when(qi - groups * ATTN_STEPS_PER_LOOP == rest)
        def _(rest=rest):
            for s in range(rest):
                step(groups * ATTN_STEPS_PER_LOOP + s, s % 2)
            update(qi, rest % 2, diagonal=True)

    lam = (jnp.exp(jnp.sum(lq1_ref[...] * lk1_ref[...], axis=-1, keepdims=True))
           - jnp.exp(jnp.sum(lq2_ref[...] * lk2_ref[...], axis=-1, keepdims=True)) + lambda_init)
    for g in range(heads):
        a1, a2 = ast[2 * g][...], ast[2 * g + 1][...]
        inv1, inv2 = 1.0 / a1[hd:hd + 1, :], 1.0 / a2[hd:hd + 1, :]
        ot = a1[0:hd, :] * inv1 - lam * (a2[0:hd, :] * inv2)
        ot = ot * lax.rsqrt(jnp.mean(ot * ot, axis=0, keepdims=True) + EPS) * gain_t[...]
        out_ref[:, cols[g]] = (ot.T * jax.nn.silu(az_ref[:, cols[g]].astype(F32))).astype(out_ref.dtype)


def _attn_small_inputs(slopes, lq1, lk1, lq2, lk2, subg):
    slope_rows = jnp.broadcast_to(slopes[:, None, None], (ATTN_HEADS, 1, 128))
    r = lambda a: a.astype(F32).reshape(1, -1)
    return slope_rows, r(lq1), r(lk1), r(lq2), r(lk2), r(subg)


def _small_specs():
    return [pl.BlockSpec((None, 1, 128), lambda b, h, *_: (h, 0, 0)),
            _const_spec((1, ATTN_HEAD_DIM)), _const_spec((1, ATTN_HEAD_DIM)),
            _const_spec((1, ATTN_HEAD_DIM)), _const_spec((1, ATTN_HEAD_DIM)),
            _const_spec((1, ATTN_V_DIM))]


def _attn_prompt(q, k, v, az, small, lambda_init):
    B, T, W = q.shape
    tq = _row_tile(T, 256)
    assert tq % CHUNK == 0 and tq <= 256
    nk = T // tq
    hd = ATTN_V_DIM
    hg = ATTN_HEAD_GROUP
    rows_acc = hd + ATTN_ONES_ROWS
    tile = pl.BlockSpec((None, tq, hg * hd), lambda b, h, i: (b, i, h))
    full = pl.BlockSpec((None, T, hg * hd), lambda b, h, i: (b, 0, h))
    small_specs = [pl.BlockSpec((hg, 1, 128), lambda b, h, i: (h, 0, 0))] + _small_specs()[1:]
    return pl.pallas_call(
        functools.partial(_attn_prompt_body, tq=tq, nk=nk, heads=hg, lambda_init=lambda_init),
        grid=(B, ATTN_HEADS // hg, nk),
        in_specs=small_specs + [tile, full, full, tile],
        out_specs=tile,
        out_shape=jax.ShapeDtypeStruct((B, T, W), BF16),
        scratch_shapes=[pltpu.VMEM((hg, T, 2 * hd), BF16),
                        pltpu.VMEM((hg, nk, rows_acc, tq), BF16),
                        pltpu.VMEM((hg, tq, tq), F32),
                        pltpu.VMEM((2 * hg, 2 * hd, tq), BF16),
                        pltpu.VMEM((hd, tq), F32)]
        + [pltpu.VMEM((1, tq), F32)] * (3 * 2 * hg) + [pltpu.VMEM((rows_acc, tq), F32)] * (2 * hg)
        + [pltpu.VMEM((tq, tq), F32)] * (2 * 2 * hg),
        compiler_params=_params("parallel", "parallel", "arbitrary"),
        name="attn_prompt",
    )(*small, q, k, v, az)


def _attn_sample_body(slope_ref, lq1_ref, lk1_ref, lq2_ref, lk2_ref, subg_ref,
                      q_ref, kp_ref, vp_ref, kn_ref, vn_ref, az_ref, out_ref, m_s, l_s, acc_s,
                      *, past, pc, lambda_init):
    j = pl.program_id(1)
    T = q_ref.shape[0]
    hd = ATTN_V_DIM
    heads = ATTN_HEADS
    cols = [slice(h * hd, (h + 1) * hd) for h in range(heads)]
    slopes = [slope_ref[h, 0:1, 0:1] for h in range(heads)]
    qpos = past + (lax.broadcasted_iota(jnp.int32, (2 * T, 1), 0) & (T - 1))
    qposf = qpos.astype(F32)

    @pl.when(j == 0)
    def _():
        m_s[...] = jnp.full(m_s.shape, NEG_INF, F32)
        l_s[...] = jnp.zeros(l_s.shape, F32)
        acc_s[...] = jnp.zeros(acc_s.shape, F32)

    qs = [jnp.concatenate(_split_q(q_ref[:, cols[h]]), axis=0) for h in range(heads)]

    def attend(keys, vals, bias_of):
        ss = [_scores(qs[h], keys[h]) + bias_of(h) for h in range(heads)]
        for h in range(heads):
            m_s[h], l_s[h], acc_s[h] = _online(ss[h], vals[h], m_s[h], l_s[h], acc_s[h])

    kposf = (j * pc + lax.broadcasted_iota(jnp.int32, (1, pc), 1)).astype(F32)
    attend([kp_ref[pl.ds(h, pc, stride=heads), :].astype(BF16) for h in range(heads)],
           [vp_ref[pl.ds(h, pc, stride=heads), :].astype(BF16) for h in range(heads)],
           lambda h: slopes[h] * kposf - slopes[h] * qposf)

    @pl.when(j == pl.num_programs(1) - 1)
    def _():
        kpos = past + lax.broadcasted_iota(jnp.int32, (1, T), 1)
        dist = jnp.abs(qpos - kpos).astype(F32)
        visible = (kpos >> 6) <= (qpos >> 6)
        attend([kn_ref[:, cols[h]].astype(BF16) for h in range(heads)],
               [vn_ref[:, cols[h]].astype(BF16) for h in range(heads)],
               lambda h: jnp.where(visible, -slopes[h] * dist, NEG_INF))
        for h in range(heads):
            o = acc_s[h] / l_s[h]
            o = _attn_finish(o[0:T], o[T:2 * T], lq1_ref[...], lk1_ref[...], lq2_ref[...], lk2_ref[...],
                             subg_ref[...], az_ref[:, cols[h]], lambda_init)
            out_ref[:, cols[h]] = o.astype(out_ref.dtype)


def _attn_sample(q, k_cache, v_cache, k_stack, v_stack, layer, az, small, lambda_init):
    B, T, W = q.shape
    depth, _, P = k_cache.shape[:3]
    assert P % CHUNK == 0 and T == CHUNK and P > 0
    hd = ATTN_V_DIM
    pc = _row_tile(P, 1024)
    tile = pl.BlockSpec((None, T, W), lambda b, j: (b, 0, 0))
    new = pl.BlockSpec((None, None, T, W), lambda b, j: (layer, b, 0, 0))
    pastb = pl.BlockSpec((None, None, pc * ATTN_HEADS, hd), lambda b, j: (layer, b, j, 0))
    small_specs = [pl.BlockSpec((ATTN_HEADS, 1, 128), lambda b, j: (0, 0, 0))] + _small_specs()[1:]
    rows = lambda a: a.reshape(depth, B, P * ATTN_HEADS, hd)
    return pl.pallas_call(
        functools.partial(_attn_sample_body, past=P, pc=pc, lambda_init=lambda_init),
        grid=(B, P // pc),
        in_specs=small_specs + [tile, pastb, pastb, new, new, tile],
        out_specs=tile,
        out_shape=jax.ShapeDtypeStruct((B, T, W), BF16),
        scratch_shapes=[pltpu.VMEM((ATTN_HEADS, 2 * T, 1), F32), pltpu.VMEM((ATTN_HEADS, 2 * T, 1), F32),
                        pltpu.VMEM((ATTN_HEADS, 2 * T, hd), F32)],
        compiler_params=_params("parallel", "arbitrary"),
        name="attn_sample",
    )(*small, q, rows(k_cache), rows(v_cache), k_stack, v_stack, az)


def _memat_body(q_ref, k_ref, v_ref, out_ref):
    q = q_ref[...]
    k = k_ref[...]
    v = v_ref[...]
    lane = lax.broadcasted_iota(jnp.int32, (1, MEM_WIDTH), 1)
    mine = [(lane >= h * MEM_HEAD_DIM) & (lane < (h + 1) * MEM_HEAD_DIM) for h in range(MEM_HEADS)]
    ss = [_scores(q, jnp.where(mine[h], k, 0.0).astype(BF16)) * (MEM_HEAD_DIM ** -0.5) for h in range(MEM_HEADS)]
    acc = jnp.zeros(out_ref.shape, F32)
    for h in range(MEM_HEADS):
        vh = jnp.where(mine[h], v, 0.0).astype(BF16)
        p = jnp.exp(ss[h] - jnp.max(ss[h], axis=-1, keepdims=True))
        a = p / jnp.sum(p, axis=-1, keepdims=True)
        acc = acc + jnp.dot(a.astype(BF16), vh, preferred_element_type=F32)
    out_ref[...] = acc.astype(out_ref.dtype)


def _memat(mq, mk, mv):
    B, T, W = mq.shape
    M = mk.shape[1]
    tm = _row_tile(T, 512)
    return pl.pallas_call(
        _memat_body,
        grid=(B, T // tm),
        in_specs=[pl.BlockSpec((None, tm, W), lambda b, i: (b, i, 0)),
                  pl.BlockSpec((None, M, W), lambda b, i: (b, 0, 0)),
                  pl.BlockSpec((None, M, W), lambda b, i: (b, 0, 0))],
        out_specs=pl.BlockSpec((None, tm, W), lambda b, i: (b, i, 0)),
        out_shape=jax.ShapeDtypeStruct((B, T, W), BF16),
        compiler_params=_params("parallel", "parallel"),
        name="memat",
    )(mq, mk, mv)


def _merge_body(x_ref, g_ref, pool_ref, ssm_ref, sz_ref, attn_ref, mem_ref,
                wp_ref, ws_ref, wa_ref, wm_ref, wg_ref, wo_ref, fg_ref, out_ref, *, final_norm):
    x = x_ref[...]
    D = x.shape[-1]
    h = _rms_rows(x, g_ref[...]).astype(BF16)
    rows = sz_ref.shape[0]
    ssm_y = jnp.concatenate([ssm_ref[:, j].reshape(rows, -1) for j in range(ssm_ref.shape[1])], axis=1)
    ssm_g = (ssm_y * jax.nn.silu(sz_ref[...].astype(F32))).astype(BF16)
    merged = jnp.zeros(x.shape, F32)
    branches = ((pool_ref[...], wp_ref), (ssm_g, ws_ref), (attn_ref[...], wa_ref), (mem_ref[...], wm_ref))
    for n, (a, w_ref) in enumerate(branches):
        gate = jax.nn.sigmoid(jnp.dot(h, wg_ref[:, n * D:(n + 1) * D], preferred_element_type=F32))
        merged = merged + gate * jnp.dot(a, w_ref[...], preferred_element_type=F32)
    y = x + jnp.dot(merged.astype(BF16), wo_ref[...], preferred_element_type=F32)
    if final_norm:
        y = _rms_rows(y, fg_ref[...])
    out_ref[...] = y


def _merge(x, g, pool_g, ssm_y, sz, attn_g, mem_o, wp, ws, wa, wm, wg, wo, fg, final_norm):
    B, T, D = x.shape
    tm = _row_tile(T, 512)
    tt = ssm_y.shape[3]
    bt = lambda wd: pl.BlockSpec((None, tm, wd), lambda b, i: (b, i, 0))
    return pl.pallas_call(
        functools.partial(_merge_body, final_norm=final_norm),
        grid=(B, T // tm),
        in_specs=[bt(D), _const_spec((1, D)), bt(POOL_WIDTH),
                  pl.BlockSpec((tm // tt, SSM_BLOCKS, None, tt, SSM_LANE_BLOCK), lambda b, i: (i, 0, b, 0, 0)),
                  bt(SSM_WIDTH),
                  bt(ATTN_WIDTH), bt(MEM_WIDTH),
                  _const_spec(wp.shape), _const_spec(ws.shape), _const_spec(wa.shape), _const_spec(wm.shape),
                  _const_spec(wg.shape), _const_spec(wo.shape), _const_spec((1, D))],
        out_specs=bt(D),
        out_shape=jax.ShapeDtypeStruct((B, T, D), F32),
        compiler_params=_params("parallel", "parallel"),
        name="merge",
    )(x, g.reshape(1, D), pool_g, ssm_y, sz, attn_g, mem_o, wp, ws, wa, wm, wg, wo, fg.reshape(1, D))


def _trunk_layer(x, layer, depth, pool_hist, ssm_re, ssm_im, caches, kv_stacks, mem_k, mem_v, lw, lambda_init,
                 final_g):
    B, T, D = x.shape
    pu, pz, su, sz, q, k_stack, k16, v_stack, v16, az, mq = _trunk_proj(
        x, lw['norm_g'], lw['w_in_main'], layer, depth, kv_stacks, scaled_q=caches is None)
    pos0 = 0 if caches is None else caches[0].shape[2]
    hist16 = jnp.concatenate([jnp.zeros((B, 1, POOL_WIDTH), F32), pool_hist], axis=1)
    pool_g, hist_new = _pool(pu, hist16, pz, lw['pool_w_bd'], lw['pool_scale'], pos0)
    ssm_y, sre, sim = _ssm(su, B, lw['ssm_bw'], lw['ssm_cw'], lw['ssm_a_re'], lw['ssm_a_im'], lw['ssm_d'],
                           lw['w_glu'], ssm_re.reshape(B, SSM_STATES), ssm_im.reshape(B, SSM_STATES))
    if caches is None:
        attn_g = _attn_prompt(q, k16, v16, az, lw['attn_small'], lambda_init)
    else:
        attn_g = _attn_sample(q, caches[0], caches[1], k_stack, v_stack, layer, az, lw['attn_small'], lambda_init)
    mem_o = _memat(mq, mem_k, mem_v)
    y = _merge(x, lw['norm_g'], pool_g, ssm_y, sz, attn_g, mem_o,
               lw['w_br_pool'], lw['w_br_ssm'], lw['w_br_attn'], lw['w_br_mem'], lw['w_in_gates'], lw['w_out'],
               final_g, layer == depth - 1)
    return (y, hist_new[:, 1:], sre.reshape(B, SSM_GROUPS, SSM_STATE), sim.reshape(B, SSM_GROUPS, SSM_STATE),
            (k_stack, v_stack))


def kernel(x_prompt, x_sample, mem_prompt, cache_attn_k, cache_attn_v, cache_mem_k, cache_mem_v, state_pool, state_ssm_re, state_ssm_im, norm_g, w_in, pool_w, pool_scale, ssm_lambda_re, ssm_lambda_im, ssm_log_dt, ssm_b_re, ssm_b_im, ssm_c_re, ssm_c_im, ssm_d, ssm_w_glu, attn_lq1, attn_lk1, attn_lq2, attn_lk2, attn_subln_g, mem_norm_g, w_mem_kv, w_br_pool, w_br_ssm, w_br_attn, w_br_mem, w_out, final_norm_g):
    depth = w_in.shape[0]
    B = x_prompt.shape[0]
    Bd = x_sample.shape[0]
    n_main = w_in.shape[-1] - N_BRANCH * x_prompt.shape[-1]
    slopes = jnp.asarray([2.0 ** (-8.0 * (h + 1) / ATTN_HEADS) for h in range(ATTN_HEADS)], F32)
    xp, xs = x_prompt, x_sample
    outs = {n: [] for n in ('mkp', 'mvp', 'poolp', 'srep', 'simp', 'pools', 'sres', 'sims')}
    kv_p = kv_s = None
    for l in range(depth):
        bw, cwm, a_re, a_im = _ssm_weights(ssm_lambda_re[l], ssm_lambda_im[l], ssm_log_dt[l],
                                           ssm_b_re[l], ssm_b_im[l], ssm_c_re[l], ssm_c_im[l])
        eye = jnp.eye(len(POOL_WINDOWS), dtype=F32)
        pool_w_bd = jnp.einsum('gcd,gh->gchd', pool_w[l].astype(F32), eye).reshape(POOL_WIDTH, POOL_WIDTH)
        lw = {
            'norm_g': norm_g[l], 'w_in_main': w_in[l, :, :n_main].astype(BF16),
            'w_in_gates': w_in[l, :, n_main:].astype(BF16),
            'pool_w_bd': pool_w_bd.astype(BF16), 'pool_scale': pool_scale[l],
            'ssm_bw': bw, 'ssm_cw': cwm, 'ssm_a_re': a_re, 'ssm_a_im': a_im, 'ssm_d': ssm_d[l],
            'w_glu': ssm_w_glu[l].astype(BF16),
            'attn_small': _attn_small_inputs(slopes, attn_lq1[l], attn_lk1[l], attn_lq2[l], attn_lk2[l],
                                             attn_subln_g[l]),
            'w_br_pool': w_br_pool[l].astype(BF16), 'w_br_ssm': w_br_ssm[l].astype(BF16),
            'w_br_attn': w_br_attn[l].astype(BF16), 'w_br_mem': w_br_mem[l].astype(BF16),
            'w_out': w_out[l].astype(BF16),
        }
        lambda_init = 0.8 - 0.6 * math.exp(-0.3 * l)
        mk_p, mv_p = _memory_kv(mem_prompt, mem_norm_g[l], w_mem_kv[l].astype(BF16))
        zero_pool = jnp.zeros((B, POOL_HIST, POOL_WIDTH), F32)
        zero_ssm = jnp.zeros((B, SSM_GROUPS, SSM_STATE), F32)
        xp, pool_p, sre_p, sim_p, kv_p = _trunk_layer(
            xp, l, depth, zero_pool, zero_ssm, zero_ssm, None, kv_p, mk_p, mv_p, lw, lambda_init, final_norm_g)
        M = cache_mem_k.shape[2]
        xs, pool_s, sre_s, sim_s, kv_s = _trunk_layer(
            xs, l, depth, state_pool[l], state_ssm_re[l], state_ssm_im[l], (cache_attn_k, cache_attn_v), kv_s,
            cache_mem_k[l].reshape(Bd, M, MEM_WIDTH), cache_mem_v[l].reshape(Bd, M, MEM_WIDTH),
            lw, lambda_init, final_norm_g)
        mshape = (B, mem_prompt.shape[1], MEM_HEADS, MEM_HEAD_DIM)
        for n, a in (('mkp', mk_p.reshape(mshape)), ('mvp', mv_p.reshape(mshape)),
                     ('poolp', pool_p), ('srep', sre_p), ('simp', sim_p),
                     ('pools', pool_s), ('sres', sre_s), ('sims', sim_s)):
            outs[n].append(a)
    st = {n: jnp.stack(a) for n, a in outs.items()}
    heads = lambda a: a.reshape(a.shape[:3] + (ATTN_HEADS, ATTN_V_DIM))
    return (xp, xs, heads(kv_p[0]), heads(kv_p[1]), st['mkp'], st['mvp'], st['poolp'], st['srep'], st['simp'],
            heads(kv_s[0]), heads(kv_s[1]), st['pools'], st['sres'], st['sims'])
```

```python
import functools
import math
import struct

import jax
import jax.numpy as jnp
from jax import lax
from jax.experimental import pallas as pl
from jax.experimental.pallas import tpu as pltpu

F32 = jnp.float32
BF16 = jnp.bfloat16

CHUNK = 64
EPS = 1e-6
NEG_INF = -1e30

POOL_WIDTH = 768
POOL_WINDOWS = (2, 4, 8, 16)
POOL_GROUP = POOL_WIDTH // len(POOL_WINDOWS)
POOL_HIST = 15
POOL_PAD = 32

SSM_WIDTH = 768
SSM_GROUP = 16
SSM_GROUPS = SSM_WIDTH // SSM_GROUP
SSM_STATE = 64
SSM_STATES = SSM_GROUPS * SSM_STATE
SSM_LANE_BLOCK = 128
SSM_BLOCKS = SSM_WIDTH // SSM_LANE_BLOCK
SSM_BLOCK_STATES = SSM_STATES // SSM_BLOCKS

ATTN_HEADS = 8
ATTN_HEAD_DIM = 64
ATTN_V_DIM = 128
ATTN_WIDTH = ATTN_HEADS * ATTN_V_DIM

MEM_HEADS = 4
MEM_HEAD_DIM = 64
MEM_WIDTH = MEM_HEADS * MEM_HEAD_DIM

N_BRANCH = 4

VMEM_LIMIT_BYTES = 56 * 1024 * 1024


def _params(*sem):
    return pltpu.CompilerParams(dimension_semantics=sem, vmem_limit_bytes=VMEM_LIMIT_BYTES)


def _const_spec(shape):
    nd = len(shape)
    return pl.BlockSpec(shape, lambda *_: (0,) * nd, pipeline_mode=pl.Buffered(1))


def _rms_rows(x, g):
    return x * lax.rsqrt(jnp.mean(x * x, axis=-1, keepdims=True) + EPS) * g


def _proj_body(x_ref, g_ref, w_ref, *out_refs, segs):
    h = _rms_rows(x_ref[...], g_ref[...]).astype(BF16)
    outs = iter(out_refs)
    for a, b, modes in segs:
        y = jnp.dot(h, w_ref[:, a:b], preferred_element_type=F32)
        for mode in modes:
            o_ref = next(outs)
            if mode == 'slots':
                for s in range(o_ref.shape[0]):
                    o_ref[s] = y.astype(o_ref.dtype)
            elif mode == 'tiles':
                nt, nl, tt, lw = o_ref.shape
                for j in range(nl):
                    o_ref[:, j] = y[:, j * lw:(j + 1) * lw].astype(o_ref.dtype).reshape(nt, tt, lw)
            elif mode == 'scaled_q':
                o_ref[...] = (y * ATTN_Q_SCALE).astype(o_ref.dtype)
            else:
                o_ref[...] = y.astype(o_ref.dtype)


def _row_tile(t, want):
    tm = min(t, want)
    assert t % tm == 0
    return tm


def _proj(x, g, w, segs, out_shapes, out_specs, tm, carried=None):
    B, T, D = x.shape
    carried = carried or {}

    def body(x_ref, g_ref, w_ref, *refs):
        _proj_body(x_ref, g_ref, w_ref, *refs[len(carried):], segs=segs)

    return pl.pallas_call(
        body,
        grid=(B, T // tm),
        in_specs=[pl.BlockSpec((None, tm, D), lambda b, i: (b, i, 0)),
                  _const_spec((1, D)), _const_spec(w.shape)]
        + [pl.BlockSpec(memory_space=pl.ANY)] * len(carried),
        out_specs=out_specs,
        out_shape=out_shapes,
        input_output_aliases={3 + n: o for n, o in enumerate(carried)},
        compiler_params=_params("parallel", "parallel"),
        name="proj",
    )(x, g.reshape(1, D), w, *carried.values())


def _ssm_time_tile(T, nb):
    return _row_tile(T, max(8, 1024 // nb))


def _trunk_proj(x, g, w, layer, depth, kv_stacks, scaled_q):
    B, T, D = x.shape
    tm = _row_tile(T, 512)
    tt = _ssm_time_tile(T, B)
    bounds = [0, 768, 1536, 2304, 3072, 4096, 5120, 6144, 7168, 7424]
    first = kv_stacks is None
    stack_mode = 'slots' if first else 'rows'
    q_mode = 'scaled_q' if scaled_q else 'rows'
    modes = [('rows',), ('rows',), ('tiles',), ('rows',), (q_mode,), (stack_mode, 'rows'), (stack_mode, 'rows'),
             ('rows',), ('rows',)]
    segs = tuple(zip(bounds[:-1], bounds[1:], modes))
    bt = lambda wd, dt: jax.ShapeDtypeStruct((B, T, wd), dt)
    s_bt = lambda wd: pl.BlockSpec((None, tm, wd), lambda b, i: (b, i, 0))
    tiles = jax.ShapeDtypeStruct((T // tt, SSM_BLOCKS, B, tt, SSM_LANE_BLOCK), F32)
    s_tiles = pl.BlockSpec((tm // tt, SSM_BLOCKS, None, tt, SSM_LANE_BLOCK), lambda b, i: (i, 0, b, 0, 0))
    stack = jax.ShapeDtypeStruct((depth, B, T, ATTN_WIDTH), F32)
    if first:
        s_stack = pl.BlockSpec((depth, None, tm, ATTN_WIDTH), lambda b, i: (0, b, i, 0))
    else:
        s_stack = pl.BlockSpec((None, None, tm, ATTN_WIDTH), lambda b, i: (layer, b, i, 0))
    shapes = [bt(768, F32), bt(768, BF16), tiles, bt(768, BF16), bt(1024, BF16),
              stack, bt(1024, BF16), stack, bt(1024, BF16), bt(1024, BF16), bt(256, BF16)]
    specs = [s_bt(768), s_bt(768), s_tiles, s_bt(768), s_bt(1024),
             s_stack, s_bt(1024), s_stack, s_bt(1024), s_bt(1024), s_bt(256)]
    carried = None if first else {5: kv_stacks[0], 7: kv_stacks[1]}
    return _proj(x, g, w, segs, shapes, specs, tm, carried)


def _memory_kv(mem, g, w):
    B, M, D = mem.shape
    segs = ((0, MEM_WIDTH, ('rows',)), (MEM_WIDTH, 2 * MEM_WIDTH, ('rows',)))
    shapes = [jax.ShapeDtypeStruct((B, M, MEM_WIDTH), F32)] * 2
    specs = [pl.BlockSpec((None, M, MEM_WIDTH), lambda b, i: (b, i, 0))] * 2
    return _proj(mem, g, w, segs, shapes, specs, M)


def _pool_body(u_ref, hist_ref, z_ref, w_ref, scale_ref, out_ref, hist_out_ref,
               ext, b2, b4, b8, *, tt, pos0):
    i = pl.program_id(1)
    n = POOL_PAD + tt

    @pl.when(i == 0)
    def _():
        ext[0:16, :] = jnp.zeros((16, POOL_WIDTH), F32)
        ext[16:32, :] = hist_ref[...]

    @pl.when(i > 0)
    def _():
        ext[16:32, :] = ext[tt + 16:tt + 32, :]

    u = u_ref[...]
    ext[32:n, :] = u
    b2[8:n, :] = ext[8:n, :] + ext[7:n - 1, :]
    b4[16:n, :] = b2[16:n, :] + b2[14:n - 2, :]
    b8[24:n, :] = b4[24:n, :] + b4[20:n - 4, :]
    s2 = b2[32:n, :]
    s4 = b4[32:n, :]
    s8 = b8[32:n, :]
    s16 = s8 + b8[24:n - 8, :]
    col = lax.broadcasted_iota(jnp.int32, (1, POOL_WIDTH), 1)
    wsum = jnp.where(col < POOL_GROUP, s2,
                     jnp.where(col < 2 * POOL_GROUP, s4, jnp.where(col < 3 * POOL_GROUP, s8, s16)))
    win = jnp.where(col < POOL_GROUP, 2, jnp.where(col < 2 * POOL_GROUP, 4, jnp.where(col < 3 * POOL_GROUP, 8, 16)))
    pos = pos0 + i * tt + lax.broadcasted_iota(jnp.int32, (tt, 1), 0)
    cnt = jnp.minimum(win, pos + 1).astype(F32)
    diff = (wsum / cnt - u).astype(BF16)
    y = jnp.dot(diff, w_ref[...], preferred_element_type=F32) * scale_ref[...]
    out_ref[...] = (y * jax.nn.silu(z_ref[...].astype(F32))).astype(out_ref.dtype)
    hist_out_ref[...] = ext[tt + 16:tt + 32, :]


def _pool(pu, hist16, pz, w_bd, scale, pos0):
    B, T, C = pu.shape
    tt = _row_tile(T, 512)
    n = POOL_PAD + tt
    return pl.pallas_call(
        functools.partial(_pool_body, tt=tt, pos0=pos0),
        grid=(B, T // tt),
        in_specs=[pl.BlockSpec((None, tt, C), lambda b, i: (b, i, 0)),
                  pl.BlockSpec((None, 16, C), lambda b, i: (b, 0, 0)),
                  pl.BlockSpec((None, tt, C), lambda b, i: (b, i, 0)),
                  _const_spec((C, C)), _const_spec((1, C))],
        out_specs=[pl.BlockSpec((None, tt, C), lambda b, i: (b, i, 0)),
                   pl.BlockSpec((None, 16, C), lambda b, i: (b, 0, 0))],
        out_shape=[jax.ShapeDtypeStruct((B, T, C), BF16), jax.ShapeDtypeStruct((B, 16, C), F32)],
        scratch_shapes=[pltpu.VMEM((n, C), F32)] * 4,
        compiler_params=_params("parallel", "arbitrary"),
        name="pool",
    )(pu, hist16, pz, w_bd, scale.reshape(1, C))


def _ssm_body(u_ref, bw_ref, cw_ref, are_ref, aim_ref, d_ref, wglu_ref, s0re_ref, s0im_ref,
              out_ref, sre_ref, sim_ref, u, bu_a, bu_b, y, *, nb, tt, cw):
    i = pl.program_id(0)

    @pl.when(i == 0)
    def _():
        sre_ref[...] = s0re_ref[...]
        sim_ref[...] = s0im_ref[...]

    for j in range(SSM_BLOCKS):
        for t in range(tt):
            u[t * nb:(t + 1) * nb, j * SSM_LANE_BLOCK:(j + 1) * SSM_LANE_BLOCK] = u_ref[j, pl.ds(t, nb, stride=tt), :]

    half = SSM_BLOCK_STATES
    for j in range(SSM_BLOCKS):
        lanes = slice(j * SSM_LANE_BLOCK, (j + 1) * SSM_LANE_BLOCK)
        bu = (bu_a, bu_b)[j % 2]
        bu[...] = jnp.dot(u[:, lanes].astype(BF16), bw_ref[j], preferred_element_type=F32)
        for c in range(half // cw):
            st = slice(j * half + c * cw, j * half + (c + 1) * cw)
            cre = slice(c * cw, (c + 1) * cw)
            cim = slice(half + c * cw, half + (c + 1) * cw)
            ar = jnp.broadcast_to(are_ref[:, st], (nb, cw))
            ai = jnp.broadcast_to(aim_ref[:, st], (nb, cw))
            hr, hi = sre_ref[:, st], sim_ref[:, st]
            for t in range(tt):
                rows = slice(t * nb, (t + 1) * nb)
                hr, hi = ar * hr - ai * hi + bu[rows, cre], ar * hi + ai * hr + bu[rows, cim]
                bu[rows, cre] = hr
                bu[rows, cim] = hi
            sre_ref[:, st] = hr
            sim_ref[:, st] = hi
        y[:, lanes] = jnp.dot(bu[...].astype(BF16), cw_ref[j], preferred_element_type=F32)

    yy = y[...] + d_ref[...] * u[...]
    act = jax.nn.gelu(yy).astype(BF16)
    zz = jnp.dot(act, wglu_ref[...], preferred_element_type=F32)
    y[...] = zz[:, :SSM_WIDTH] * jax.nn.sigmoid(zz[:, SSM_WIDTH:])
    for j in range(SSM_BLOCKS):
        for t in range(tt):
            out_ref[j, pl.ds(t, nb, stride=tt), :] = y[t * nb:(t + 1) * nb, j * SSM_LANE_BLOCK:(j + 1) * SSM_LANE_BLOCK]


def _ssm(su, nb, bw, cw_mat, a_re, a_im, d, w_glu, s0_re, s0_im):
    nt, nl, _, tt, lw = su.shape
    C = nl * lw
    rows = tt * nb
    cw = min(SSM_BLOCK_STATES, 128 * max(1, 64 // nb))
    tile = pl.BlockSpec((None, nl, rows, lw), lambda i: (i, 0, 0, 0))
    y, sre, sim = pl.pallas_call(
        functools.partial(_ssm_body, nb=nb, tt=tt, cw=cw),
        grid=(nt,),
        in_specs=[tile, _const_spec(bw.shape), _const_spec(cw_mat.shape),
                  _const_spec((1, SSM_STATES)), _const_spec((1, SSM_STATES)),
                  _const_spec((1, C)), _const_spec(w_glu.shape),
                  _const_spec((nb, SSM_STATES)), _const_spec((nb, SSM_STATES))],
        out_specs=[tile,
                   pl.BlockSpec((nb, SSM_STATES), lambda i: (0, 0)),
                   pl.BlockSpec((nb, SSM_STATES), lambda i: (0, 0))],
        out_shape=[jax.ShapeDtypeStruct((nt, nl, rows, lw), F32),
                   jax.ShapeDtypeStruct((nb, SSM_STATES), F32),
                   jax.ShapeDtypeStruct((nb, SSM_STATES), F32)],
        scratch_shapes=[pltpu.VMEM((rows, C), F32), pltpu.VMEM((rows, 2 * SSM_BLOCK_STATES), F32),
                        pltpu.VMEM((rows, 2 * SSM_BLOCK_STATES), F32), pltpu.VMEM((rows, C), F32)],
        compiler_params=_params("arbitrary"),
        name="ssm",
    )(su.reshape(nt, nl, rows, lw), bw, cw_mat, a_re.reshape(1, -1), a_im.reshape(1, -1), d.reshape(1, C), w_glu,
      s0_re, s0_im)
    return y.reshape(su.shape), sre, sim


def _ssm_weights(lam_re, lam_im, log_dt, b_re, b_im, c_re, c_im):
    dt = jnp.exp(log_dt.astype(F32))[:, None]
    lr, li = lam_re.astype(F32), lam_im.astype(F32)
    mag = jnp.exp(lr * dt)
    ab_re, ab_im = mag * jnp.cos(li * dt), mag * jnp.sin(li * dt)
    nr, ni = ab_re - 1.0, ab_im
    den = lr * lr + li * li
    f_re = (nr * lr + ni * li) / den
    f_im = (ni * lr - nr * li) / den
    br, bi = b_re.astype(F32), b_im.astype(F32)
    bb_re = f_re[..., None] * br - f_im[..., None] * bi
    bb_im = f_re[..., None] * bi + f_im[..., None] * br
    gpb = SSM_GROUPS // SSM_BLOCKS
    eye = jnp.eye(gpb, dtype=F32)

    def in_block(bb):
        bb = bb.reshape(SSM_BLOCKS, gpb, SSM_STATE, SSM_GROUP)
        m = jnp.einsum('jgpc,gh->jgchp', bb, eye)
        return m.reshape(SSM_BLOCKS, gpb * SSM_GROUP, gpb * SSM_STATE)

    def out_block(cc):
        cc = cc.reshape(SSM_BLOCKS, gpb, SSM_GROUP, SSM_STATE)
        m = jnp.einsum('jgcp,gh->jgphc', cc, eye)
        return m.reshape(SSM_BLOCKS, gpb * SSM_STATE, gpb * SSM_GROUP)

    bw = jnp.concatenate([in_block(bb_re), in_block(bb_im)], axis=-1).astype(BF16)
    cwm = jnp.concatenate([out_block(c_re.astype(F32)), -out_block(c_im.astype(F32))], axis=1).astype(BF16)
    return bw, cwm, ab_re.reshape(-1), ab_im.reshape(-1)


def _split_q(q):
    qf = q.astype(F32) * (ATTN_HEAD_DIM ** -0.5)
    lane = lax.broadcasted_iota(jnp.int32, (1, ATTN_V_DIM), 1)
    first = lane < ATTN_HEAD_DIM
    return jnp.where(first, qf, 0.0).astype(BF16), jnp.where(first, 0.0, qf).astype(BF16)


def _scores(q, k):
    return lax.dot_general(q, k, (((1,), (1,)), ((), ())), preferred_element_type=F32)


def _online(t, v, m, l, acc):
    m_new = jnp.maximum(m, jnp.max(t, axis=-1, keepdims=True))
    alpha = jnp.exp(m - m_new)
    p = jnp.exp(t - m_new)
    l_new = alpha * l + jnp.sum(p, axis=-1, keepdims=True)
    acc_new = alpha * acc + jnp.dot(p.astype(BF16), v, preferred_element_type=F32)
    return m_new, l_new, acc_new


def _attn_finish(o1, o2, lq1, lk1, lq2, lk2, subg, az, lambda_init):
    lam = (jnp.exp(jnp.sum(lq1 * lk1, axis=-1, keepdims=True))
           - jnp.exp(jnp.sum(lq2 * lk2, axis=-1, keepdims=True)) + lambda_init)
    o = o1 - lam * o2
    o = _rms_rows(o, subg) * (1.0 - lambda_init)
    return o * jax.nn.silu(az.astype(F32))


ATTN_ONES_ROWS = 16
ATTN_HEAD_GROUP = 4
ATTN_STEPS_PER_LOOP = 4


def _bf16_parts(x, n):
    parts = []
    for _ in range(n):
        bits = struct.unpack('<I', struct.pack('<f', x))[0]
        bits = (bits + 0x7FFF + ((bits >> 16) & 1)) & 0xFFFF0000
        part = struct.unpack('<f', struct.pack('<I', bits))[0]
        parts.append(part)
        x -= part
    return parts


LOG2E = math.log2(math.e)
LOG2E_PARTS = _bf16_parts(LOG2E, 3)
ATTN_Q_SCALE = ATTN_HEAD_DIM ** -0.5 * LOG2E


def _attn_prompt_body(slope_ref, lq1_ref, lk1_ref, lq2_ref, lk2_ref, subg_ref,
                      q_ref, k_ref, v_ref, az_ref, out_ref, kx, vxt, corr, wq, gain_t, *state,
                      tq, nk, heads, lambda_init):
    qi = pl.program_id(2)
    hd = ATTN_V_DIM
    rows_acc = hd + ATTN_ONES_ROWS
    lane = lax.broadcasted_iota(jnp.int32, (tq, hd), 1)
    row = lax.broadcasted_iota(jnp.int32, (tq, hd), 0).astype(F32)
    slopes = [slope_ref[g, 0:1, 0:1] for g in range(heads)]
    cols = [slice(g * hd, (g + 1) * hd) for g in range(heads)]
    nc = 2 * heads
    mst, tmax_a, tmax_b, ast, t_a, t_b = (state[n * nc:(n + 1) * nc] for n in range(6))
    tmax_s, t_s = (tmax_a, tmax_b), (t_a, t_b)

    @pl.when(qi == 0)
    def _():
        ones = jnp.ones((ATTN_ONES_ROWS, tq), BF16)
        gain_t[...] = jnp.broadcast_to(subg_ref[...] * (1.0 - lambda_init), (tq, hd)).T
        c = lax.broadcasted_iota(jnp.int32, (tq, tq), 0)
        r = lax.broadcasted_iota(jnp.int32, (tq, tq), 1)
        visible = (c >> 6) <= (r >> 6)
        ahead = jnp.where(c > r, (c - r).astype(F32), 0.0)
        for g in range(heads):
            kfeat = jnp.where(lane < 3, row, 0.0)
            for n, part in enumerate(LOG2E_PARTS):
                kfeat = jnp.where(lane == 3 + n, -slopes[g] * part, kfeat)
            kfeat = kfeat.astype(BF16)
            for j in range(nk):
                rows = slice(j * tq, (j + 1) * tq)
                kx[g, rows, 0:hd] = k_ref[rows, cols[g]]
                kx[g, rows, hd:2 * hd] = kfeat
                vxt[g, j, 0:hd, :] = v_ref[rows, cols[g]].astype(F32).T.astype(BF16)
                vxt[g, j, hd:rows_acc, :] = ones
            corr[g] = jnp.where(visible, (-2.0 * LOG2E) * slopes[g] * ahead, NEG_INF)
            frow = lax.broadcasted_iota(jnp.int32, (hd, tq), 0)
            qfeat = jnp.where((frow >= 3) & (frow < 6), lax.broadcasted_iota(jnp.int32, (hd, tq), 1).astype(F32), 0.0)
            for n, part in enumerate(LOG2E_PARTS):
                qfeat = jnp.where(frow == n, slopes[g] * part, qfeat)
            zeros = jnp.zeros((ATTN_HEAD_DIM, tq), BF16)
            for n in range(2):
                wq[2 * g + n, hd:2 * hd, :] = qfeat.astype(BF16)
            wq[2 * g, ATTN_HEAD_DIM:hd, :] = zeros
            wq[2 * g + 1, 0:ATTN_HEAD_DIM, :] = zeros

    half = ATTN_HEAD_DIM
    for g in range(heads):
        qt = q_ref[:, cols[g]].astype(F32).T.astype(BF16)
        wq[2 * g, 0:half, :] = qt[0:half]
        wq[2 * g + 1, half:hd, :] = qt[half:hd]
    chains = [(g, 2 * g + n) for g in range(heads) for n in range(2)]
    for _, c in chains:
        mst[c][...] = jnp.full((1, tq), NEG_INF, F32)
        ast[c][...] = jnp.zeros((rows_acc, tq), F32)

    def scores(j, buf):
        kstart = pl.multiple_of(j * tq, tq)
        for g, c in chains:
            t = jnp.dot(kx[g, pl.ds(kstart, tq), :], wq[c], preferred_element_type=F32)
            t_s[buf][c][...] = t
            tmax_s[buf][c][...] = jnp.max(t, axis=0, keepdims=True)

    def update(j, buf, diagonal=False):
        for g, c in chains:
            if diagonal:
                t = t_s[buf][c][...] + corr[g]
                tmax = jnp.max(t, axis=0, keepdims=True)
                shift = jnp.zeros((1, 1), F32)
            else:
                t, tmax = t_s[buf][c][...], tmax_s[buf][c][...]
                shift = (-LOG2E) * slopes[g] * ((qi - j) * tq).astype(F32)
            m_old = mst[c][...]
            m_new = jnp.maximum(m_old, tmax + shift)
            mst[c][...] = m_new
            p = jnp.exp2(t + (shift - m_new)).astype(BF16)
            ast[c][...] = jnp.exp2(m_old - m_new) * ast[c][...] + jnp.dot(vxt[g, j], p,
                                                                         preferred_element_type=F32)

    def step(j, buf):
        scores(j + 1, 1 - buf)
        update(j, buf)

    def group(k, carry):
        for s in range(ATTN_STEPS_PER_LOOP):
            step(ATTN_STEPS_PER_LOOP * k + s, (s + 1) % 2)
        return carry

    scores(qi, 0)

    @pl.when(qi == 0)
    def _():
        update(qi, 0, diagonal=True)

    @pl.when(qi > 0)
    def _():
        scores(0, 1)
        update(qi, 0, diagonal=True)

    steps = jnp.maximum(qi - 1, 0)
    groups = steps // ATTN_STEPS_PER_LOOP
    lax.fori_loop(0, groups, group, 0)
    for rest in range(ATTN_STEPS_PER_LOOP):
        @pl.when((qi > 0) & (steps - groups * ATTN_STEPS_PER_LOOP == rest))
        def _(rest=rest):
            for s in range(rest):
                step(groups * ATTN_STEPS_PER_LOOP + s, (s + 1) % 2)
            update(qi - 1, (rest + 1) % 2)

    lam = (jnp.exp(jnp.sum(lq1_ref[...] * lk1_ref[...], axis=-1, keepdims=True))
           - jnp.exp(jnp.sum(lq2_ref[...] * lk2_ref[...], axis=-1, keepdims=True)) + lambda_init)
    for g in range(heads):
        a1, a2 = ast[2 * g][...], ast[2 * g + 1][...]
        inv1, inv2 = 1.0 / a1[hd:hd + 1, :], 1.0 / a2[hd:hd + 1, :]
        ot = a1[0:hd, :] * inv1 - lam * (a2[0:hd, :] * inv2)
        ot = ot * lax.rsqrt(jnp.mean(ot * ot, axis=0, keepdims=True) + EPS) * gain_t[...]
        out_ref[:, cols[g]] = (ot.T * jax.nn.silu(az_ref[:, cols[g]].astype(F32))).astype(out_ref.dtype)


def _attn_small_inputs(slopes, lq1, lk1, lq2, lk2, subg):
    slope_rows = jnp.broadcast_to(slopes[:, None, None], (ATTN_HEADS, 1, 128))
    r = lambda a: a.astype(F32).reshape(1, -1)
    return slope_rows, r(lq1), r(lk1), r(lq2), r(lk2), r(subg)


def _small_specs():
    return [pl.BlockSpec((None, 1, 128), lambda b, h, *_: (h, 0, 0)),
            _const_spec((1, ATTN_HEAD_DIM)), _const_spec((1, ATTN_HEAD_DIM)),
            _const_spec((1, ATTN_HEAD_DIM)), _const_spec((1, ATTN_HEAD_DIM)),
            _const_spec((1, ATTN_V_DIM))]


def _attn_prompt(q, k, v, az, small, lambda_init):
    B, T, W = q.shape
    tq = _row_tile(T, 256)
    assert tq % CHUNK == 0 and tq <= 256
    nk = T // tq
    hd = ATTN_V_DIM
    hg = ATTN_HEAD_GROUP
    rows_acc = hd + ATTN_ONES_ROWS
    tile = pl.BlockSpec((None, tq, hg * hd), lambda b, h, i: (b, i, h))
    full = pl.BlockSpec((None, T, hg * hd), lambda b, h, i: (b, 0, h))
    small_specs = [pl.BlockSpec((hg, 1, 128), lambda b, h, i: (h, 0, 0))] + _small_specs()[1:]
    return pl.pallas_call(
        functools.partial(_attn_prompt_body, tq=tq, nk=nk, heads=hg, lambda_init=lambda_init),
        grid=(B, ATTN_HEADS // hg, nk),
        in_specs=small_specs + [tile, full, full, tile],
        out_specs=tile,
        out_shape=jax.ShapeDtypeStruct((B, T, W), BF16),
        scratch_shapes=[pltpu.VMEM((hg, T, 2 * hd), BF16),
                        pltpu.VMEM((hg, nk, rows_acc, tq), BF16),
                        pltpu.VMEM((hg, tq, tq), F32),
                        pltpu.VMEM((2 * hg, 2 * hd, tq), BF16),
                        pltpu.VMEM((hd, tq), F32)]
        + [pltpu.VMEM((1, tq), F32)] * (3 * 2 * hg) + [pltpu.VMEM((rows_acc, tq), F32)] * (2 * hg)
        + [pltpu.VMEM((tq, tq), F32)] * (2 * 2 * hg),
        compiler_params=_params("parallel", "parallel", "arbitrary"),
        name="attn_prompt",
    )(*small, q, k, v, az)


def _attn_sample_body(slope_ref, lq1_ref, lk1_ref, lq2_ref, lk2_ref, subg_ref,
                      q_ref, kp_ref, vp_ref, kn_ref, vn_ref, az_ref, out_ref, m_s, l_s, acc_s,
                      *, past, pc, lambda_init):
    j = pl.program_id(1)
    T = q_ref.shape[0]
    hd = ATTN_V_DIM
    heads = ATTN_HEADS
    cols = [slice(h * hd, (h + 1) * hd) for h in range(heads)]
    slopes = [slope_ref[h, 0:1, 0:1] for h in range(heads)]
    qpos = past + (lax.broadcasted_iota(jnp.int32, (2 * T, 1), 0) & (T - 1))
    qposf = qpos.astype(F32)

    @pl.when(j == 0)
    def _():
        m_s[...] = jnp.full(m_s.shape, NEG_INF, F32)
        l_s[...] = jnp.zeros(l_s.shape, F32)
        acc_s[...] = jnp.zeros(acc_s.shape, F32)

    qs = [jnp.concatenate(_split_q(q_ref[:, cols[h]]), axis=0) for h in range(heads)]

    def attend(keys, vals, bias_of):
        ss = [_scores(qs[h], keys[h]) + bias_of(h) for h in range(heads)]
        for h in range(heads):
            m_s[h], l_s[h], acc_s[h] = _online(ss[h], vals[h], m_s[h], l_s[h], acc_s[h])

    kposf = (j * pc + lax.broadcasted_iota(jnp.int32, (1, pc), 1)).astype(F32)
    attend([kp_ref[pl.ds(h, pc, stride=heads), :].astype(BF16) for h in range(heads)],
           [vp_ref[pl.ds(h, pc, stride=heads), :].astype(BF16) for h in range(heads)],
           lambda h: slopes[h] * kposf - slopes[h] * qposf)

    @pl.when(j == pl.num_programs(1) - 1)
    def _():
        kpos = past + lax.broadcasted_iota(jnp.int32, (1, T), 1)
        dist = jnp.abs(qpos - kpos).astype(F32)
        visible = (kpos >> 6) <= (qpos >> 6)
        attend([kn_ref[:, cols[h]].astype(BF16) for h in range(heads)],
               [vn_ref[:, cols[h]].astype(BF16) for h in range(heads)],
               lambda h: jnp.where(visible, -slopes[h] * dist, NEG_INF))
        for h in range(heads):
            o = acc_s[h] / l_s[h]
            o = _attn_finish(o[0:T], o[T:2 * T], lq1_ref[...], lk1_ref[...], lq2_ref[...], lk2_ref[...],
                             subg_ref[...], az_ref[:, cols[h]], lambda_init)
            out_ref[:, cols[h]] = o.astype(out_ref.dtype)


def _attn_sample(q, k_cache, v_cache, k_stack, v_stack, layer, az, small, lambda_init):
    B, T, W = q.shape
    depth, _, P = k_cache.shape[:3]
    assert P % CHUNK == 0 and T == CHUNK and P > 0
    hd = ATTN_V_DIM
    pc = _row_tile(P, 1024)
    tile = pl.BlockSpec((None, T, W), lambda b, j: (b, 0, 0))
    new = pl.BlockSpec((None, None, T, W), lambda b, j: (layer, b, 0, 0))
    pastb = pl.BlockSpec((None, None, pc * ATTN_HEADS, hd), lambda b, j: (layer, b, j, 0))
    small_specs = [pl.BlockSpec((ATTN_HEADS, 1, 128), lambda b, j: (0, 0, 0))] + _small_specs()[1:]
    rows = lambda a: a.reshape(depth, B, P * ATTN_HEADS, hd)
    return pl.pallas_call(
        functools.partial(_attn_sample_body, past=P, pc=pc, lambda_init=lambda_init),
        grid=(B, P // pc),
        in_specs=small_specs + [tile, pastb, pastb, new, new, tile],
        out_specs=tile,
        out_shape=jax.ShapeDtypeStruct((B, T, W), BF16),
        scratch_shapes=[pltpu.VMEM((ATTN_HEADS, 2 * T, 1), F32), pltpu.VMEM((ATTN_HEADS, 2 * T, 1), F32),
                        pltpu.VMEM((ATTN_HEADS, 2 * T, hd), F32)],
        compiler_params=_params("parallel", "arbitrary"),
        name="attn_sample",
    )(*small, q, rows(k_cache), rows(v_cache), k_stack, v_stack, az)


def _memat_body(q_ref, k_ref, v_ref, out_ref):
    q = q_ref[...]
    k = k_ref[...]
    v = v_ref[...]
    lane = lax.broadcasted_iota(jnp.int32, (1, MEM_WIDTH), 1)
    mine = [(lane >= h * MEM_HEAD_DIM) & (lane < (h + 1) * MEM_HEAD_DIM) for h in range(MEM_HEADS)]
    ss = [_scores(q, jnp.where(mine[h], k, 0.0).astype(BF16)) * (MEM_HEAD_DIM ** -0.5) for h in range(MEM_HEADS)]
    acc = jnp.zeros(out_ref.shape, F32)
    for h in range(MEM_HEADS):
        vh = jnp.where(mine[h], v, 0.0).astype(BF16)
        p = jnp.exp(ss[h] - jnp.max(ss[h], axis=-1, keepdims=True))
        a = p / jnp.sum(p, axis=-1, keepdims=True)
        acc = acc + jnp.dot(a.astype(BF16), vh, preferred_element_type=F32)
    out_ref[...] = acc.astype(out_ref.dtype)


def _memat(mq, mk, mv):
    B, T, W = mq.shape
    M = mk.shape[1]
    tm = _row_tile(T, 512)
    return pl.pallas_call(
        _memat_body,
        grid=(B, T // tm),
        in_specs=[pl.BlockSpec((None, tm, W), lambda b, i: (b, i, 0)),
                  pl.BlockSpec((None, M, W), lambda b, i: (b, 0, 0)),
                  pl.BlockSpec((None, M, W), lambda b, i: (b, 0, 0))],
        out_specs=pl.BlockSpec((None, tm, W), lambda b, i: (b, i, 0)),
        out_shape=jax.ShapeDtypeStruct((B, T, W), BF16),
        compiler_params=_params("parallel", "parallel"),
        name="memat",
    )(mq, mk, mv)


def _merge_body(x_ref, g_ref, pool_ref, ssm_ref, sz_ref, attn_ref, mem_ref,
                wp_ref, ws_ref, wa_ref, wm_ref, wg_ref, wo_ref, fg_ref, out_ref, *, final_norm):
    x = x_ref[...]
    D = x.shape[-1]
    h = _rms_rows(x, g_ref[...]).astype(BF16)
    rows = sz_ref.shape[0]
    ssm_y = jnp.concatenate([ssm_ref[:, j].reshape(rows, -1) for j in range(ssm_ref.shape[1])], axis=1)
    ssm_g = (ssm_y * jax.nn.silu(sz_ref[...].astype(F32))).astype(BF16)
    merged = jnp.zeros(x.shape, F32)
    branches = ((pool_ref[...], wp_ref), (ssm_g, ws_ref), (attn_ref[...], wa_ref), (mem_ref[...], wm_ref))
    for n, (a, w_ref) in enumerate(branches):
        gate = jax.nn.sigmoid(jnp.dot(h, wg_ref[:, n * D:(n + 1) * D], preferred_element_type=F32))
        merged = merged + gate * jnp.dot(a, w_ref[...], preferred_element_type=F32)
    y = x + jnp.dot(merged.astype(BF16), wo_ref[...], preferred_element_type=F32)
    if final_norm:
        y = _rms_rows(y, fg_ref[...])
    out_ref[...] = y


def _merge(x, g, pool_g, ssm_y, sz, attn_g, mem_o, wp, ws, wa, wm, wg, wo, fg, final_norm):
    B, T, D = x.shape
    tm = _row_tile(T, 512)
    tt = ssm_y.shape[3]
    bt = lambda wd: pl.BlockSpec((None, tm, wd), lambda b, i: (b, i, 0))
    return pl.pallas_call(
        functools.partial(_merge_body, final_norm=final_norm),
        grid=(B, T // tm),
        in_specs=[bt(D), _const_spec((1, D)), bt(POOL_WIDTH),
                  pl.BlockSpec((tm // tt, SSM_BLOCKS, None, tt, SSM_LANE_BLOCK), lambda b, i: (i, 0, b, 0, 0)),
                  bt(SSM_WIDTH),
                  bt(ATTN_WIDTH), bt(MEM_WIDTH),
                  _const_spec(wp.shape), _const_spec(ws.shape), _const_spec(wa.shape), _const_spec(wm.shape),
                  _const_spec(wg.shape), _const_spec(wo.shape), _const_spec((1, D))],
        out_specs=bt(D),
        out_shape=jax.ShapeDtypeStruct((B, T, D), F32),
        compiler_params=_params("parallel", "parallel"),
        name="merge",
    )(x, g.reshape(1, D), pool_g, ssm_y, sz, attn_g, mem_o, wp, ws, wa, wm, wg, wo, fg.reshape(1, D))


def _trunk_layer(x, layer, depth, pool_hist, ssm_re, ssm_im, caches, kv_stacks, mem_k, mem_v, lw, lambda_init,
                 final_g):
    B, T, D = x.shape
    pu, pz, su, sz, q, k_stack, k16, v_stack, v16, az, mq = _trunk_proj(
        x, lw['norm_g'], lw['w_in_main'], layer, depth, kv_stacks, scaled_q=caches is None)
    pos0 = 0 if caches is None else caches[0].shape[2]
    hist16 = jnp.concatenate([jnp.zeros((B, 1, POOL_WIDTH), F32), pool_hist], axis=1)
    pool_g, hist_new = _pool(pu, hist16, pz, lw['pool_w_bd'], lw['pool_scale'], pos0)
    ssm_y, sre, sim = _ssm(su, B, lw['ssm_bw'], lw['ssm_cw'], lw['ssm_a_re'], lw['ssm_a_im'], lw['ssm_d'],
                           lw['w_glu'], ssm_re.reshape(B, SSM_STATES), ssm_im.reshape(B, SSM_STATES))
    if caches is None:
        attn_g = _attn_prompt(q, k16, v16, az, lw['attn_small'], lambda_init)
    else:
        attn_g = _attn_sample(q, caches[0], caches[1], k_stack, v_stack, layer, az, lw['attn_small'], lambda_init)
    mem_o = _memat(mq, mem_k, mem_v)
    y = _merge(x, lw['norm_g'], pool_g, ssm_y, sz, attn_g, mem_o,
               lw['w_br_pool'], lw['w_br_ssm'], lw['w_br_attn'], lw['w_br_mem'], lw['w_in_gates'], lw['w_out'],
               final_g, layer == depth - 1)
    return (y, hist_new[:, 1:], sre.reshape(B, SSM_GROUPS, SSM_STATE), sim.reshape(B, SSM_GROUPS, SSM_STATE),
            (k_stack, v_stack))


def kernel(x_prompt, x_sample, mem_prompt, cache_attn_k, cache_attn_v, cache_mem_k, cache_mem_v, state_pool, state_ssm_re, state_ssm_im, norm_g, w_in, pool_w, pool_scale, ssm_lambda_re, ssm_lambda_im, ssm_log_dt, ssm_b_re, ssm_b_im, ssm_c_re, ssm_c_im, ssm_d, ssm_w_glu, attn_lq1, attn_lk1, attn_lq2, attn_lk2, attn_subln_g, mem_norm_g, w_mem_kv, w_br_pool, w_br_ssm, w_br_attn, w_br_mem, w_out, final_norm_g):
    depth = w_in.shape[0]
    B = x_prompt.shape[0]
    Bd = x_sample.shape[0]
    n_main = w_in.shape[-1] - N_BRANCH * x_prompt.shape[-1]
    slopes = jnp.asarray([2.0 ** (-8.0 * (h + 1) / ATTN_HEADS) for h in range(ATTN_HEADS)], F32)
    xp, xs = x_prompt, x_sample
    outs = {n: [] for n in ('mkp', 'mvp', 'poolp', 'srep', 'simp', 'pools', 'sres', 'sims')}
    kv_p = kv_s = None
    for l in range(depth):
        bw, cwm, a_re, a_im = _ssm_weights(ssm_lambda_re[l], ssm_lambda_im[l], ssm_log_dt[l],
                                           ssm_b_re[l], ssm_b_im[l], ssm_c_re[l], ssm_c_im[l])
        eye = jnp.eye(len(POOL_WINDOWS), dtype=F32)
        pool_w_bd = jnp.einsum('gcd,gh->gchd', pool_w[l].astype(F32), eye).reshape(POOL_WIDTH, POOL_WIDTH)
        lw = {
            'norm_g': norm_g[l], 'w_in_main': w_in[l, :, :n_main].astype(BF16),
            'w_in_gates': w_in[l, :, n_main:].astype(BF16),
            'pool_w_bd': pool_w_bd.astype(BF16), 'pool_scale': pool_scale[l],
            'ssm_bw': bw, 'ssm_cw': cwm, 'ssm_a_re': a_re, 'ssm_a_im': a_im, 'ssm_d': ssm_d[l],
            'w_glu': ssm_w_glu[l].astype(BF16),
            'attn_small': _attn_small_inputs(slopes, attn_lq1[l], attn_lk1[l], attn_lq2[l], attn_lk2[l],
                                             attn_subln_g[l]),
            'w_br_pool': w_br_pool[l].astype(BF16), 'w_br_ssm': w_br_ssm[l].astype(BF16),
            'w_br_attn': w_br_attn[l].astype(BF16), 'w_br_mem': w_br_mem[l].astype(BF16),
            'w_out': w_out[l].astype(BF16),
        }
        lambda_init = 0.8 - 0.6 * math.exp(-0.3 * l)
        mk_p, mv_p = _memory_kv(mem_prompt, mem_norm_g[l], w_mem_kv[l].astype(BF16))
        zero_pool = jnp.zeros((B, POOL_HIST, POOL_WIDTH), F32)
        zero_ssm = jnp.zeros((B, SSM_GROUPS, SSM_STATE), F32)
        xp, pool_p, sre_p, sim_p, kv_p = _trunk_layer(
            xp, l, depth, zero_pool, zero_ssm, zero_ssm, None, kv_p, mk_p, mv_p, lw, lambda_init, final_norm_g)
        M = cache_mem_k.shape[2]
        xs, pool_s, sre_s, sim_s, kv_s = _trunk_layer(
            xs, l, depth, state_pool[l], state_ssm_re[l], state_ssm_im[l], (cache_attn_k, cache_attn_v), kv_s,
            cache_mem_k[l].reshape(Bd, M, MEM_WIDTH), cache_mem_v[l].reshape(Bd, M, MEM_WIDTH),
            lw, lambda_init, final_norm_g)
        mshape = (B, mem_prompt.shape[1], MEM_HEADS, MEM_HEAD_DIM)
        for n, a in (('mkp', mk_p.reshape(mshape)), ('mvp', mv_p.reshape(mshape)),
                     ('poolp', pool_p), ('srep', sre_p), ('simp', sim_p),
                     ('pools', pool_s), ('sres', sre_s), ('sims', sim_s)):
            outs[n].append(a)
    st = {n: jnp.stack(a) for n, a in outs.items()}
    heads = lambda a: a.reshape(a.shape[:3] + (ATTN_HEADS, ATTN_V_DIM))
    return (xp, xs, heads(kv_p[0]), heads(kv_p[1]), st['mkp'], st['mvp'], st['poolp'], st['srep'], st['simp'],
            heads(kv_s[0]), heads(kv_s[1]), st['pools'], st['sres'], st['sims'])
```
